```python
import jax, jax.numpy as jnp
from jax import lax
import numpy as np

D_MODEL = 2048
BATCH = 2
SEQ = 4096
DEPTH = 4
DEC_BATCH = 8
DEC_SEQ = 1
PAST_LEN = 16384
PAGE_SIZE = 128

A_GROUPS = 4
A_CH = 128
A_WIDTH = A_GROUPS * A_CH
A_CHUNK = 128
B_HEADS = 6
B_DK = 128
B_DV = 128
B_WIDTH = B_HEADS * B_DV
CONV_W = 4
DN_CHUNK = 64
C_HEADS = 6
C_HD = 128
C_WIDTH = C_HEADS * C_HD
SB_BLOCK = 128
SB_BIAS_INIT = -6.0
MIX_WIDTH = A_WIDTH + B_WIDTH + C_WIDTH
D_FF = ((8 * D_MODEL + 3 * 256 - 1) // (3 * 256)) * 256
EPS = 1e-6

IN_SIZES = (A_WIDTH, A_WIDTH, B_WIDTH, B_WIDTH, B_WIDTH, B_WIDTH, B_HEADS, B_HEADS,
            C_WIDTH, C_WIDTH, C_WIDTH, D_MODEL, D_MODEL, D_MODEL)
IN_WIDTH = int(sum(IN_SIZES))
IN_SPLITS = tuple(np.cumsum(IN_SIZES)[:-1].tolist())

kernel_name = 'hybrid_gmlp_gdn_stickbreak_step'


def _rms(x, g):
    xf = x.astype(jnp.float32)
    y = xf * lax.rsqrt(jnp.mean(xf * xf, axis=-1, keepdims=True) + EPS)
    return (y * g.astype(jnp.float32)).astype(x.dtype)


def _l2n(x):
    return x * lax.rsqrt(jnp.sum(x * x, axis=-1, keepdims=True) + EPS)


def _chunk_mlp(u, v, g_v, w_s, b_s):
    Bsz, T, _ = v.shape
    L = A_CHUNK if T % A_CHUNK == 0 else T
    n = T // L
    vn = _rms(v.reshape(Bsz, T, A_GROUPS, A_CH), g_v)
    w = w_s[:, :L, :L] * jnp.tril(jnp.ones((L, L), w_s.dtype))
    s = jnp.einsum('gts,bnsgc->bntgc', w, vn.reshape(Bsz, n, L, A_GROUPS, A_CH))
    s = s + b_s[:, :L].T[None, None, :, :, None]
    return u * s.reshape(Bsz, T, A_WIDTH), vn.reshape(Bsz, T, A_WIDTH)


def _short_conv(x, buf, w):
    T = x.shape[1]
    xp = jnp.concatenate([buf, x], axis=1)
    y = sum(w[i] * xp[:, i:i + T] for i in range(CONV_W))
    return jax.nn.silu(y), xp[:, T:]


def _gdn_chunk(S, inp):
    q, k, v, beta, g = inp
    L = q.shape[2]
    G = jnp.cumsum(g, axis=-1)
    incl = jnp.tril(jnp.ones((L, L), bool))
    strict = jnp.tril(jnp.ones((L, L), bool), -1)
    D = jnp.exp(jnp.where(incl, G[..., :, None] - G[..., None, :], -jnp.inf))
    M = jnp.where(strict, beta[..., :, None] * jnp.einsum('bhtd,bhjd->bhtj', k, k) * D, 0.0)
    gam = jnp.exp(G)
    rhs = jnp.concatenate([beta[..., None] * v, (beta * gam)[..., None] * k], axis=-1)
    sol = lax.linalg.triangular_solve(M + jnp.eye(L, dtype=M.dtype), rhs,
                                      left_side=True, lower=True, unit_diagonal=True)
    U = sol[..., :B_DV] - jnp.einsum('bhtk,bhkv->bhtv', sol[..., B_DV:], S)
    o = (jnp.einsum('bhtk,bhkv->bhtv', gam[..., None] * q, S)
         + jnp.einsum('bhtj,bhjv->bhtv', jnp.einsum('bhtd,bhjd->bhtj', q, k) * D, U))
    G_last = G[..., -1:]
    S_new = (jnp.exp(G_last)[..., None] * S
             + jnp.einsum('bhjk,bhjv->bhkv', k * jnp.exp(G_last - G)[..., None], U))
    return S_new, o


def _gated_delta(q, k, v, z, a, bt, conv_buf, S0, conv_w, a_log, dt_bias, g_o):
    Bsz, T, _ = q.shape
    dt = z.dtype
    qkv, new_buf = _short_conv(jnp.concatenate([q, k, v], axis=-1), conv_buf, conv_w)
    q, k, v = jnp.split(qkv, 3, axis=-1)

    def heads(t, d):
        return t.reshape(Bsz, T, B_HEADS, d).astype(jnp.float32).transpose(0, 2, 1, 3)

    qh = _l2n(heads(q, B_DK)) * (B_DK ** -0.5)
    kh = _l2n(heads(k, B_DK))
    vh = heads(v, B_DV)
    beta = jax.nn.sigmoid(bt.astype(jnp.float32)).transpose(0, 2, 1)
    g = (-jnp.exp(a_log.astype(jnp.float32))
         * jax.nn.softplus(a.astype(jnp.float32) + dt_bias.astype(jnp.float32))).transpose(0, 2, 1)
    L = DN_CHUNK if T % DN_CHUNK == 0 else T
    n = T // L

    def chunks(t):
        return jnp.moveaxis(t.reshape(t.shape[:2] + (n, L) + t.shape[3:]), 2, 0)

    S, o = lax.scan(_gdn_chunk, S0.astype(jnp.float32),
                    (chunks(qh), chunks(kh), chunks(vh), chunks(beta), chunks(g)))
    o = jnp.moveaxis(o, 0, 2).reshape(Bsz, B_HEADS, T, B_DV).transpose(0, 2, 1, 3)
    o = _rms(o, g_o) * jax.nn.silu(z.reshape(Bsz, T, B_HEADS, B_DV).astype(jnp.float32))
    return o.reshape(Bsz, T, B_WIDTH).astype(dt), S.astype(S0.dtype), new_buf


def _sb_block(qb, qpos, keys, vals, kpos, bias):
    z = (jnp.einsum('bqhd,bkhd->bhqk', qb, keys).astype(jnp.float32) * (C_HD ** -0.5)
         + bias.astype(jnp.float32)[None, :, None, None])
    mask = kpos[None, :] < qpos[:, None]
    c = jnp.where(mask, jax.nn.log_sigmoid(-z), 0.0)
    tail = lax.cumsum(c, axis=3, reverse=True) - c
    w = jnp.where(mask, jnp.exp(jax.nn.log_sigmoid(z) + tail), 0.0)
    return jnp.einsum('bhqk,bkhd->bqhd', w.astype(vals.dtype), vals)


def _stick_breaking(q, keys, vals, q_offset, bias):
    Bsz, T = q.shape[:2]
    kpos = jnp.arange(keys.shape[1])
    Lq = SB_BLOCK if T % SB_BLOCK == 0 else T
    nb = T // Lq

    def blk(i):
        qb = lax.dynamic_slice_in_dim(q, i * Lq, Lq, axis=1)
        qpos = q_offset + i * Lq + jnp.arange(Lq)
        return _sb_block(qb, qpos, keys, vals, kpos, bias)

    out = lax.map(blk, jnp.arange(nb))
    return jnp.moveaxis(out, 0, 1).reshape(Bsz, T, C_WIDTH)


def _layer(x, p, conv_buf, S0, past_k, past_v):
    Bsz, T, _ = x.shape
    h = _rms(x, p['norm_mix'])
    proj = h @ p['w_in']
    (a_u, a_v, b_q, b_k, b_v, b_z, b_a, b_b, c_q, c_k, c_v, g_a, g_b, g_c) = jnp.split(proj, IN_SPLITS, axis=-1)
    ya, a_rows = _chunk_mlp(jax.nn.gelu(a_u), jax.nn.gelu(a_v), p['a_vnorm'], p['a_ws'], p['a_bs'])
    yb, S, new_buf = _gated_delta(b_q, b_k, b_v, b_z, b_a, b_b, conv_buf, S0,
                                  p['dn_conv'], p['dn_a_log'], p['dn_dt_bias'], p['dn_onorm'])
    qc = _rms(c_q.reshape(Bsz, T, C_HEADS, C_HD), p['sb_qnorm'])
    kc = _rms(c_k.reshape(Bsz, T, C_HEADS, C_HD), p['sb_knorm'])
    vc = c_v.reshape(Bsz, T, C_HEADS, C_HD)
    if past_k is None:
        keys, vals, off = kc, vc, 0
    else:
        keys = jnp.concatenate([past_k, kc], axis=1)
        vals = jnp.concatenate([past_v, vc], axis=1)
        off = past_k.shape[1]
    yc = _stick_breaking(qc, keys, vals, off, p['sb_bias'])
    wb = p['w_branch']
    m = (jax.nn.sigmoid(g_a) * (ya @ wb[:A_WIDTH])
         + jax.nn.sigmoid(g_b) * (yb @ wb[A_WIDTH:A_WIDTH + B_WIDTH])
         + jax.nn.sigmoid(g_c) * (yc @ wb[A_WIDTH + B_WIDTH:]))
    x = x + m @ p['w_out']
    h2 = _rms(x, p['norm_ffn'])
    x = x + (jax.nn.silu(h2 @ p['w_gate']) * (h2 @ p['w_up'])) @ p['w_down']
    return x, (kc, vc, S, new_buf, a_rows)


def setup_inputs(seed: int = 0) -> dict:
    key = jax.random.key(seed)
    ks = jax.random.split(key, 32)
    f32 = jnp.float32
    n_pages = PAST_LEN // PAGE_SIZE
    n_pool = (DEC_BATCH * n_pages * 5 + 3) // 4
    nrm = lambda k, s, sc: jax.random.normal(k, s, f32) * sc
    gain = lambda k, s: 1.0 + 0.02 * jax.random.normal(k, s, f32)
    page_table = jax.random.permutation(ks[0], n_pool)[:DEC_BATCH * n_pages]
    page_table = page_table.reshape(DEC_BATCH, n_pages).astype(jnp.int32)
    dt0 = jnp.exp(jax.random.uniform(ks[1], (DEPTH, B_HEADS), f32, np.log(1e-3), np.log(1e-1)))
    return {
        'x_prompt': nrm(ks[2], (BATCH, SEQ, D_MODEL), 1.0),
        'x_sample': nrm(ks[3], (DEC_BATCH, DEC_SEQ, D_MODEL), 1.0),
        'cache_k': nrm(ks[4], (DEPTH, n_pool, PAGE_SIZE, C_HEADS, C_HD), 1.0),
        'cache_v': nrm(ks[5], (DEPTH, n_pool, PAGE_SIZE, C_HEADS, C_HD), 1.0),
        'state_delta': nrm(ks[6], (DEPTH, DEC_BATCH, B_HEADS, B_DK, B_DV), B_DK ** -0.5),
        'state_conv': nrm(ks[7], (DEPTH, DEC_BATCH, CONV_W - 1, 3 * B_WIDTH), 1.0),
        'page_table': page_table,
        'norm_mix': gain(ks[8], (DEPTH, D_MODEL)),
        'w_in': nrm(ks[9], (DEPTH, D_MODEL, IN_WIDTH), D_MODEL ** -0.5),
        'a_vnorm': gain(ks[10], (DEPTH, A_GROUPS, A_CH)),
        'a_ws': nrm(ks[11], (DEPTH, A_GROUPS, A_CHUNK, A_CHUNK), A_CHUNK ** -0.5),
        'a_bs': gain(ks[12], (DEPTH, A_GROUPS, A_CHUNK)),
        'dn_conv': nrm(ks[13], (DEPTH, CONV_W, 3 * B_WIDTH), CONV_W ** -0.5),
        'dn_a_log': jnp.log(jax.random.uniform(ks[14], (DEPTH, B_HEADS), f32, 1.0, 16.0)),
        'dn_dt_bias': dt0 + jnp.log(-jnp.expm1(-dt0)),
        'dn_onorm': gain(ks[15], (DEPTH, B_DV)),
        'sb_qnorm': gain(ks[16], (DEPTH, C_HD)),
        'sb_knorm': gain(ks[17], (DEPTH, C_HD)),
        'sb_bias': SB_BIAS_INIT + 0.5 * jax.random.normal(ks[24], (DEPTH, C_HEADS), f32),
        'w_branch': nrm(ks[18], (DEPTH, MIX_WIDTH, D_MODEL), MIX_WIDTH ** -0.5),
        'w_out': nrm(ks[19], (DEPTH, D_MODEL, D_MODEL), D_MODEL ** -0.5),
        'norm_ffn': gain(ks[20], (DEPTH, D_MODEL)),
        'w_gate': nrm(ks[21], (DEPTH, D_MODEL, D_FF), D_MODEL ** -0.5),
        'w_up': nrm(ks[22], (DEPTH, D_MODEL, D_FF), D_MODEL ** -0.5),
        'w_down': nrm(ks[23], (DEPTH, D_FF, D_MODEL), D_FF ** -0.5),
    }


def reference(x_prompt, x_sample, cache_k, cache_v, state_delta, state_conv, page_table,
              norm_mix, w_in, a_vnorm, a_ws, a_bs, dn_conv, dn_a_log, dn_dt_bias, dn_onorm,
              sb_qnorm, sb_knorm, sb_bias, w_branch, w_out, norm_ffn, w_gate, w_up, w_down):
    Bp = x_prompt.shape[0]
    Bs = x_sample.shape[0]
    past = page_table.shape[1] * cache_k.shape[2]
    yp, ys = x_prompt, x_sample
    kp_l, vp_l, sp_l, cp_l = [], [], [], []
    ks_l, vs_l, ss_l, cs_l, as_l = [], [], [], [], []
    for l in range(DEPTH):
        p = {'norm_mix': norm_mix[l], 'w_in': w_in[l], 'a_vnorm': a_vnorm[l], 'a_ws': a_ws[l],
             'a_bs': a_bs[l], 'dn_conv': dn_conv[l], 'dn_a_log': dn_a_log[l],
             'dn_dt_bias': dn_dt_bias[l], 'dn_onorm': dn_onorm[l], 'sb_qnorm': sb_qnorm[l],
             'sb_knorm': sb_knorm[l], 'sb_bias': sb_bias[l], 'w_branch': w_branch[l],
             'w_out': w_out[l], 'norm_ffn': norm_ffn[l], 'w_gate': w_gate[l], 'w_up': w_up[l],
             'w_down': w_down[l]}
        zero_buf = jnp.zeros((Bp, CONV_W - 1, 3 * B_WIDTH), x_prompt.dtype)
        zero_S = jnp.zeros((Bp, B_HEADS, B_DK, B_DV), state_delta.dtype)
        yp, (kp, vp, Sp, bp, _) = _layer(yp, p, zero_buf, zero_S, None, None)
        pk = cache_k[l][page_table].reshape(Bs, past, C_HEADS, C_HD)
        pv = cache_v[l][page_table].reshape(Bs, past, C_HEADS, C_HD)
        ys, (k_s, v_s, S_s, b_s, a_s) = _layer(ys, p, state_conv[l], state_delta[l], pk, pv)
        kp_l.append(kp); vp_l.append(vp); sp_l.append(Sp); cp_l.append(bp)
        ks_l.append(k_s); vs_l.append(v_s); ss_l.append(S_s); cs_l.append(b_s); as_l.append(a_s)
    return (yp, ys, jnp.stack(kp_l), jnp.stack(vp_l), jnp.stack(sp_l), jnp.stack(cp_l),
            jnp.stack(ks_l), jnp.stack(vs_l), jnp.stack(ss_l), jnp.stack(cs_l), jnp.stack(as_l))
```

```python
import functools

import jax
import jax.numpy as jnp
from jax import lax
from jax.experimental import pallas as pl
from jax.experimental.pallas import tpu as pltpu

F32 = jnp.float32
BF16 = jnp.bfloat16
EPS = 1e-6
LANES = 128
SUBLANES = 8
A_GROUPS = 4
A_WIDTH = A_GROUPS * LANES
A_CHUNK = 128
HEADS = 6
HW = HEADS * LANES
CONV_W = 4
GDN_CHUNK = 128
SB_TQ = 256
SB_TK = 256
PAGES_PER_STEP = 4
VMEM_LIMIT_BYTES = 56 * 1024 * 1024


def _tile(n, candidates):
    for c in candidates:
        if n % c == 0:
            return c
    return n


def _cparams(*sem):
    return pltpu.CompilerParams(dimension_semantics=sem, vmem_limit_bytes=VMEM_LIMIT_BYTES)


def _dot(a, b):
    return jnp.dot(a, b, preferred_element_type=F32)


def _dot_nt(a, b):
    return lax.dot_general(a, b, (((1,), (1,)), ((), ())), preferred_element_type=F32)


def _sigmoid(x):
    return 1.0 / (1.0 + jnp.exp(-x))


def _silu(x):
    return x * _sigmoid(x)


def _softplus(x):
    return jnp.maximum(x, 0.0) + jnp.log(1.0 + jnp.exp(-jnp.abs(x)))


def _log_sigmoid(x):
    return jnp.minimum(x, 0.0) - jnp.log(1.0 + jnp.exp(-jnp.abs(x)))


def _rms_rows(x):
    return x * lax.rsqrt(jnp.mean(x * x, axis=-1, keepdims=True) + EPS)


def _l2n_rows(x):
    return x * lax.rsqrt(jnp.sum(x * x, axis=-1, keepdims=True) + EPS)


def _rms_cast_kernel(x_ref, g_ref, o_ref):
    o_ref[...] = (_rms_rows(x_ref[...]) * g_ref[...]).astype(o_ref.dtype)


def _rms_cast(x, gains, l):
    R, D = x.shape
    tr = _tile(R, (512, 256, 128, SUBLANES))
    return pl.pallas_call(
        _rms_cast_kernel, grid=(R // tr,),
        in_specs=[pl.BlockSpec((tr, D), lambda i: (i, 0)),
                  pl.BlockSpec((None, 1, D), lambda i: (l, 0, 0))],
        out_specs=pl.BlockSpec((tr, D), lambda i: (i, 0)),
        out_shape=jax.ShapeDtypeStruct((R, D), BF16),
        compiler_params=_cparams("parallel"), name="rms_cast")(x, gains)


def _epi_identity(acc, o_ref):
    o_ref[...] = acc.astype(o_ref.dtype)


def _epi_gelu(acc, o_ref):
    o_ref[...] = jax.nn.gelu(acc).astype(o_ref.dtype)


def _epi_silu(acc, o_ref):
    o_ref[...] = _silu(acc).astype(o_ref.dtype)


def _epi_sigmoid(acc, o_ref):
    o_ref[...] = _sigmoid(acc).astype(o_ref.dtype)


def _epi_group_rms(acc, gain_ref, o_ref, *, gelu):
    for c in range(acc.shape[1] // LANES):
        cs = slice(c * LANES, (c + 1) * LANES)
        blk = acc[:, cs]
        if gelu:
            blk = jax.nn.gelu(blk)
        o_ref[:, cs] = (_rms_rows(blk) * gain_ref[:, cs]).astype(o_ref.dtype)


def _proj_kernel(h_ref, w_ref, *rest, epilogue):
    epilogue(_dot(h_ref[...], w_ref[...]), *rest)


def _proj(h, w, l, epilogue, gain=None, out_dtype=F32):
    R, K = h.shape
    N = w.shape[-1]
    tm = _tile(R, (512, 256, 128, SUBLANES))
    tn = _tile(N, (1024, 768, 512, 384, 256, LANES))
    in_specs = [pl.BlockSpec((tm, K), lambda j, i: (i, 0)),
                pl.BlockSpec((None, K, tn), lambda j, i: (l, 0, j))]
    args = [h, w]
    if gain is not None:
        in_specs.append(pl.BlockSpec((None, 1, tn), lambda j, i: (l, 0, j)))
        args.append(gain)
    return pl.pallas_call(
        functools.partial(_proj_kernel, epilogue=epilogue), grid=(N // tn, R // tm),
        in_specs=in_specs, out_specs=pl.BlockSpec((tm, tn), lambda j, i: (i, j)),
        out_shape=jax.ShapeDtypeStruct((R, N), out_dtype),
        compiler_params=_cparams("parallel", "parallel"), name="proj")(*args)


def _mixer_a_kernel(gu_ref, vn_ref, ws_ref, bs_ref, o_ref, *, chunks):
    t = lax.broadcasted_iota(jnp.int32, (A_CHUNK, A_CHUNK), 0)
    s = lax.broadcasted_iota(jnp.int32, (A_CHUNK, A_CHUNK), 1)
    for g in range(A_GROUPS):
        cs = slice(g * LANES, (g + 1) * LANES)
        w = jnp.where(t >= s, ws_ref[g], 0.0).astype(BF16)
        for c in range(chunks):
            rs = slice(c * A_CHUNK, (c + 1) * A_CHUNK)
            mixed = _dot(w, vn_ref[rs, cs].astype(BF16)) + bs_ref[:, cs]
            o_ref[rs, cs] = (gu_ref[rs, cs] * mixed).astype(o_ref.dtype)


def _mixer_a(gu, vn, a_ws, bs_rows, l):
    R = gu.shape[0]
    chunks = _tile(R // A_CHUNK, (4, 2, 1))
    tr = chunks * A_CHUNK
    return pl.pallas_call(
        functools.partial(_mixer_a_kernel, chunks=chunks), grid=(R // tr,),
        in_specs=[pl.BlockSpec((tr, A_WIDTH), lambda i: (i, 0)),
                  pl.BlockSpec((tr, A_WIDTH), lambda i: (i, 0)),
                  pl.BlockSpec((None, A_GROUPS, A_CHUNK, A_CHUNK), lambda i: (l, 0, 0, 0)),
                  pl.BlockSpec((None, A_CHUNK, A_WIDTH), lambda i: (l, 0, 0))],
        out_specs=pl.BlockSpec((tr, A_WIDTH), lambda i: (i, 0)),
        out_shape=jax.ShapeDtypeStruct((R, A_WIDTH), F32),
        compiler_params=_cparams("parallel"), name="mixer_a")(gu, vn, a_ws, bs_rows)


def _split3_bf16(x):
    a = x.astype(BF16)
    r = x - a.astype(F32)
    b = r.astype(BF16)
    c = (r - b.astype(F32)).astype(BF16)
    return a, b, c


def _gdn_prep_kernel(x_ref, halo_ref, cw_ref, ab_ref, alog_ref, dtb_ref,
                     q_ref, k_ref, v_ref, gb_ref, bb_ref, grow_ref, xp_ref, *, tr):
    i = pl.program_id(1)
    xp_ref[0:SUBLANES, :] = jnp.where(i == 0, 0.0, halo_ref[...])
    xp_ref[SUBLANES:, :] = x_ref[...]
    y = cw_ref[CONV_W - 1:CONV_W, :] * x_ref[...]
    for s in range(1, CONV_W):
        y = y + cw_ref[CONV_W - 1 - s:CONV_W - s, :] * xp_ref[pl.ds(SUBLANES - s, tr), :]
    y = _silu(y)
    for h in range(HEADS):
        hs = slice(h * LANES, (h + 1) * LANES)
        q_ref[:, hs] = _l2n_rows(y[:, hs]) * (LANES ** -0.5)
        k_ref[:, hs] = _l2n_rows(y[:, HW + h * LANES:HW + (h + 1) * LANES])
    v_ref[...] = y[:, 2 * HW:]
    ab = ab_ref[...]
    g = -jnp.exp(alog_ref[...]) * _softplus(ab + dtb_ref[...])
    beta = _sigmoid(ab)
    r = lax.broadcasted_iota(jnp.int32, (tr, tr), 0)
    c = lax.broadcasted_iota(jnp.int32, (tr, tr), 1)
    tri = jnp.where((r >= c) & (r // GDN_CHUNK == c // GDN_CHUNK), 1.0, 0.0).astype(BF16)
    gcum = sum(_dot(tri, part) for part in _split3_bf16(g))
    for h in range(HEADS):
        hs = slice(h * LANES, (h + 1) * LANES)
        gb_ref[:, hs] = jnp.broadcast_to(gcum[:, h:h + 1], (tr, LANES))
        bb_ref[:, hs] = jnp.broadcast_to(beta[:, HEADS + h:HEADS + h + 1], (tr, LANES))
    gt = gcum.T
    for h in range(HEADS):
        grow_ref[h] = gt[h:h + 1, :]


def _gdn_prep(bqkv, ab, conv_w, alog_rows, dtb_rows, B, T, l):
    tr = _tile(T, (256, GDN_CHUNK))
    nb = T // tr
    row = lambda b, i: (b * nb + i, 0)
    halo = lambda b, i: (jnp.maximum((b * T + i * tr) // SUBLANES - 1, 0), 0)
    par = lambda b, i: (l, 0, 0)
    R = B * T
    wide = jax.ShapeDtypeStruct((R, HW), F32)
    return pl.pallas_call(
        functools.partial(_gdn_prep_kernel, tr=tr), grid=(B, nb),
        in_specs=[pl.BlockSpec((tr, 3 * HW), row),
                  pl.BlockSpec((SUBLANES, 3 * HW), halo),
                  pl.BlockSpec((None, CONV_W, 3 * HW), par),
                  pl.BlockSpec((tr, LANES), row),
                  pl.BlockSpec((None, 1, LANES), par),
                  pl.BlockSpec((None, 1, LANES), par)],
        out_specs=[pl.BlockSpec((tr, HW), row)] * 5
        + [pl.BlockSpec((None, HEADS, 1, tr), lambda b, i: (b, 0, 0, i))],
        out_shape=[wide] * 5 + [jax.ShapeDtypeStruct((B, HEADS, 1, T), F32)],
        scratch_shapes=[pltpu.VMEM((tr + SUBLANES, 3 * HW), F32)],
        compiler_params=_cparams("parallel", "parallel"), name="gdn_prep",
    )(bqkv, bqkv, conv_w, ab, alog_rows, dtb_rows)


INV_BASE = 16


def _unit_lower_inverse_minus_eye(n, r, c):
    L = n.shape[0]
    shift = INV_BASE.bit_length() - 1
    nb = jnp.where((r >> shift) == (c >> shift), n, 0.0)
    xo = nb
    nbb = nb.astype(BF16)
    p = _dot(nbb, nbb)
    for _ in range(shift - 2):
        both = _dot(jnp.concatenate([xo, p], axis=0).astype(BF16), p.astype(BF16))
        xo = xo + p + both[:L]
        p = both[L:]
    xo = xo + p + _dot(xo.astype(BF16), p.astype(BF16))
    while (1 << shift) < L:
        off = jnp.where(((r >> shift) ^ (c >> shift)) == 1, n, 0.0)
        xb = xo.astype(BF16)
        y = off + _dot(xb, off.astype(BF16))
        xo = xo + y + _dot(y.astype(BF16), xb)
        shift += 1
    return xo


def _gdn_chunk_update(q, k, v, gc, bt, grow, state):
    L = q.shape[0]
    r = lax.broadcasted_iota(jnp.int32, (L, L), 0)
    c = lax.broadcasted_iota(jnp.int32, (L, L), 1)
    decay = jnp.where(r >= c, jnp.exp(gc - grow), 0.0)
    kb = k.astype(BF16)
    kq = _dot_nt(jnp.concatenate([k, q], axis=0).astype(BF16), kb)
    kk, qk = kq[:L], kq[L:]
    n = jnp.where(r > c, -(bt * kk * decay), 0.0)
    xo = _unit_lower_inverse_minus_eye(n, r, c)
    gam = jnp.exp(gc)
    bv = bt * v
    bgk = bt * gam * k
    corr = _dot(xo.astype(BF16), jnp.concatenate([bv, bgk], axis=1).astype(BF16))
    uhat = bv + corr[:, :LANES]
    w = bgk + corr[:, LANES:]
    wq = _dot(jnp.concatenate([w, gam * q], axis=0).astype(BF16), state.astype(BF16))
    u = (uhat - wq[:L]).astype(BF16)
    glast = gc[L - 1:L, :]
    kd = (k * jnp.exp(glast - gc)).astype(BF16)
    out = wq[L:] + _dot((qk * decay).astype(BF16), u)
    new_state = jnp.exp(glast) * state + lax.dot_general(
        kd, u, (((0,), (0,)), ((), ())), preferred_element_type=F32)
    return out, new_state


def _gdn_kernel(q_ref, k_ref, v_ref, gb_ref, bb_ref, grow_ref, z_ref, go_ref, y_ref, s_ref, *, B):
    @pl.when(pl.program_id(0) == 0)
    def _():
        s_ref[...] = jnp.zeros_like(s_ref)

    for b in range(B):
        for h in range(HEADS):
            hs = slice(h * LANES, (h + 1) * LANES)
            out, new_state = _gdn_chunk_update(
                q_ref[b, :, hs], k_ref[b, :, hs], v_ref[b, :, hs], gb_ref[b, :, hs], bb_ref[b, :, hs],
                grow_ref[b, h], s_ref[b, h])
            s_ref[b, h] = new_state
            y_ref[b, :, hs] = _rms_rows(out) * go_ref[...] * z_ref[b, :, hs]


def _gdn(q, k, v, gb, bb, grow, zsilu, go_rows, B, T, l):
    L = GDN_CHUNK
    blk = pl.BlockSpec((B, L, HW), lambda i: (0, i, 0))
    three = lambda a: a.reshape(B, T, HW)
    return pl.pallas_call(
        functools.partial(_gdn_kernel, B=B), grid=(T // L,),
        in_specs=[blk] * 5 + [pl.BlockSpec((B, HEADS, 1, L), lambda i: (0, 0, 0, i)), blk,
                              pl.BlockSpec((None, 1, LANES), lambda i: (l, 0, 0))],
        out_specs=[blk, pl.BlockSpec((B, HEADS, LANES, LANES), lambda i: (0, 0, 0, 0))],
        out_shape=[jax.ShapeDtypeStruct((B, T, HW), F32),
                   jax.ShapeDtypeStruct((B, HEADS, LANES, LANES), F32)],
        compiler_params=_cparams("arbitrary"), name="gdn",
    )(three(q), three(k), three(v), three(gb), three(bb), grow, three(zsilu), go_rows)


def _sb_block(q, k_ref, v_ref, bias, jj, carry, acc, *, diagonal):
    tq, tk = SB_TQ, SB_TK
    ks = pl.ds(pl.multiple_of(jj * tk, tk), tk)
    z = _dot_nt(q, k_ref[ks, :].astype(BF16)) + bias
    ls = _log_sigmoid(z)
    c = ls - z
    r = lax.broadcasted_iota(jnp.int32, (tk, tk), 0)
    s = lax.broadcasted_iota(jnp.int32, (tk, tk), 1)
    if diagonal:
        mask = s < r
        c = jnp.where(mask, c, 0.0)
    later = jnp.where(r > s, 1.0, 0.0).astype(BF16)
    tail = _dot(c.astype(BF16), later)
    w = jnp.exp(ls + tail + carry)
    if diagonal:
        w = jnp.where(mask, w, 0.0)
    acc = acc + _dot(w.astype(BF16), v_ref[ks, :].astype(BF16))
    carry = carry + tail[:, :1] + c[:, :1]
    return carry, acc


def _sb_kernel(q_ref, k_ref, v_ref, bias_ref, o_ref):
    i = pl.program_id(2)
    q = (q_ref[...] * (LANES ** -0.5)).astype(BF16)
    bias = bias_ref[...]
    carry = jnp.zeros((SB_TQ, 1), F32)
    acc = jnp.zeros((SB_TQ, LANES), F32)
    carry, acc = _sb_block(q, k_ref, v_ref, bias, i, carry, acc, diagonal=True)

    def body(step, state):
        return _sb_block(q, k_ref, v_ref, bias, i - 1 - step, *state, diagonal=False)

    carry, acc = lax.fori_loop(0, i, body, (carry, acc))
    o_ref[...] = acc


def _sb_attention(qc, kc, vc, bias_rows, B, T, l):
    assert SB_TQ == SB_TK and T % SB_TQ == 0
    three = lambda a: a.reshape(B, T, HW)
    kv = pl.BlockSpec((None, T, LANES), lambda b, h, i: (b, 0, h))
    qo = pl.BlockSpec((None, SB_TQ, LANES), lambda b, h, i: (b, i, h))
    return pl.pallas_call(
        _sb_kernel, grid=(B, HEADS, T // SB_TQ),
        in_specs=[qo, kv, kv, pl.BlockSpec((None, None, 1, SB_TK), lambda b, h, i: (l, h, 0, 0))],
        out_specs=qo, out_shape=jax.ShapeDtypeStruct((B, T, HW), F32),
        compiler_params=_cparams("parallel", "parallel", "parallel"), name="sb_attention",
    )(three(qc), three(kc), three(vc), bias_rows)


def _sb_decode_kernel(pt_ref, q_ref, bias_ref, *rest):
    del pt_ref
    k_refs = rest[:PAGES_PER_STEP]
    v_refs = rest[PAGES_PER_STEP:2 * PAGES_PER_STEP]
    o_ref, carry_ref = rest[2 * PAGES_PER_STEP:]
    j = pl.program_id(1)

    @pl.when(j == 0)
    def _():
        o_ref[...] = jnp.zeros_like(o_ref)
        carry_ref[...] = jnp.zeros_like(carry_ref)

    q = q_ref[...] * (LANES ** -0.5)
    bias = bias_ref[:, :1]
    acc = o_ref[...]
    carry = carry_ref[:, :1]
    for p in range(PAGES_PER_STEP):
        k = k_refs[p][...]
        P = k.shape[0]
        z = jnp.sum(k * q[None], axis=-1, keepdims=True) + bias[None]
        ls = _log_sigmoid(z)
        c = ls - z
        suffix = c
        d = 1
        while d < P:
            suffix = suffix + jnp.concatenate([suffix[d:], jnp.zeros((d,) + c.shape[1:], F32)], axis=0)
            d *= 2
        w = jnp.exp(ls + (suffix - c) + carry[None])
        acc = acc + jnp.sum(w * v_refs[p][...], axis=0)
        carry = carry + suffix[0]
    o_ref[...] = acc
    carry_ref[...] = jnp.broadcast_to(carry, carry_ref.shape)


def _sb_decode(qc, cache_k, cache_v, page_table, bias_lanes, l):
    Bs, NP = page_table.shape
    P = cache_k.shape[2]
    assert NP % PAGES_PER_STEP == 0
    pt = page_table.reshape(-1)

    def page_spec(p):
        def index(b, j, pt_ref):
            return (l, pt_ref[b * NP + (NP - 1 - (j * PAGES_PER_STEP + p))], 0, 0, 0)
        return pl.BlockSpec((None, None, P, HEADS, LANES), index)

    row = pl.BlockSpec((None, HEADS, LANES), lambda b, j, pt_ref: (b, 0, 0))
    pages = [page_spec(p) for p in range(PAGES_PER_STEP)]
    return pl.pallas_call(
        _sb_decode_kernel,
        grid_spec=pltpu.PrefetchScalarGridSpec(
            num_scalar_prefetch=1, grid=(Bs, NP // PAGES_PER_STEP),
            in_specs=[row, pl.BlockSpec((None, HEADS, LANES), lambda b, j, pt_ref: (l, 0, 0))] + pages + pages,
            out_specs=row, scratch_shapes=[pltpu.VMEM((HEADS, LANES), F32)]),
        out_shape=jax.ShapeDtypeStruct((Bs, HEADS, LANES), F32),
        compiler_params=_cparams("parallel", "arbitrary"), name="sb_decode",
    )(pt, qc.reshape(Bs, HEADS, LANES), bias_lanes, *([cache_k] * PAGES_PER_STEP), *([cache_v] * PAGES_PER_STEP))


def _columns(x):
    pad = jnp.zeros((LANES - x.shape[0], LANES), F32)
    return jnp.concatenate([x, pad], axis=0).T


def _sample_mix_kernel(gu_ref, vn_ref, ws_ref, bs_ref, x_ref, hist_ref, cw_ref, ab_ref, alog_ref, dtb_ref,
                       z_ref, s_ref, go_ref, ya_ref, yb_ref, sn_ref, o_scr):
    Bs = gu_ref.shape[0]
    ya_ref[...] = gu_ref[...] * (ws_ref[...] * vn_ref[...] + bs_ref[...])
    y = cw_ref[CONV_W - 1:CONV_W, :] * x_ref[...]
    for s in range(CONV_W - 1):
        y = y + cw_ref[s:s + 1, :] * hist_ref[s]
    y = _silu(y)
    ab = ab_ref[...]
    decay = jnp.exp(-jnp.exp(alog_ref[...]) * _softplus(ab + dtb_ref[...]))
    beta = _sigmoid(ab)
    for h in range(HEADS):
        hs = slice(h * LANES, (h + 1) * LANES)
        q = _l2n_rows(y[:, hs]) * (LANES ** -0.5)
        k = _l2n_rows(y[:, HW + h * LANES:HW + (h + 1) * LANES])
        v = y[:, 2 * HW + h * LANES:2 * HW + (h + 1) * LANES]
        qcols, kcols = _columns(q), _columns(k)
        for b in range(Bs):
            a = decay[b:b + 1, h:h + 1]
            bt = beta[b:b + 1, HEADS + h:HEADS + h + 1]
            state = s_ref[b, h]
            kcol = kcols[:, b:b + 1]
            ks = jnp.sum(kcol * state, axis=0, keepdims=True)
            u = bt * (v[b:b + 1, :] - a * ks)
            new_state = a * state + kcol * u
            sn_ref[b, h] = new_state
            o_scr[b:b + 1, hs] = jnp.sum(qcols[:, b:b + 1] * new_state, axis=0, keepdims=True)
    for h in range(HEADS):
        hs = slice(h * LANES, (h + 1) * LANES)
        yb_ref[:, hs] = _rms_rows(o_scr[:, hs]) * go_ref[...] * z_ref[:, hs]


def _sample_mix(gu, vn, ws_rows, bs_rows, bqkv, hist, conv_w, ab, alog_rows, dtb_rows, zsilu, state, go_rows, l):
    Bs = gu.shape[0]
    full = lambda shape: pl.BlockSpec(shape, lambda i: (0,) * len(shape))
    par = lambda shape: pl.BlockSpec((None,) + shape, lambda i: (l,) + (0,) * len(shape))
    return pl.pallas_call(
        _sample_mix_kernel, grid=(1,),
        in_specs=[full((Bs, A_WIDTH)), full((Bs, A_WIDTH)), par((1, A_WIDTH)), par((1, A_WIDTH)),
                  full((Bs, 3 * HW)), par((CONV_W - 1, Bs, 3 * HW)), par((CONV_W, 3 * HW)),
                  full((Bs, LANES)), par((1, LANES)), par((1, LANES)), full((Bs, HW)),
                  par((Bs, HEADS, LANES, LANES)), par((1, LANES))],
        out_specs=[full((Bs, A_WIDTH)), full((Bs, HW)), full((Bs, HEADS, LANES, LANES))],
        out_shape=[jax.ShapeDtypeStruct((Bs, A_WIDTH), F32), jax.ShapeDtypeStruct((Bs, HW), F32),
                   jax.ShapeDtypeStruct((Bs, HEADS, LANES, LANES), F32)],
        scratch_shapes=[pltpu.VMEM((Bs, HW), F32)],
        compiler_params=_cparams("arbitrary"), name="sample_mix",
    )(gu, vn, ws_rows, bs_rows, bqkv, hist, conv_w, ab, alog_rows, dtb_rows, zsilu, state, go_rows)


def _merge_kernel(ya_ref, yb_ref, yc_ref, ga_ref, gb_ref, gc_ref, wa_ref, wb_ref, wc_ref, o_ref):
    m = (ga_ref[...] * _dot(ya_ref[...].astype(BF16), wa_ref[...])
         + gb_ref[...] * _dot(yb_ref[...].astype(BF16), wb_ref[...])
         + gc_ref[...] * _dot(yc_ref[...].astype(BF16), wc_ref[...]))
    o_ref[...] = m.astype(o_ref.dtype)


def _merge(ya, yb, yc, gates, wa, wb, wc, l):
    R = ya.shape[0]
    D = wa.shape[-1]
    tm = _tile(R, (512, 256, 128, SUBLANES))
    tn = _tile(D, (1024, 512, 256, LANES))
    nj = D // tn
    rows = lambda width: pl.BlockSpec((tm, width), lambda j, i: (i, 0))
    gate = lambda k: pl.BlockSpec((tm, tn), lambda j, i: (i, k * nj + j))
    wspec = lambda width: pl.BlockSpec((None, width, tn), lambda j, i: (l, 0, j))
    return pl.pallas_call(
        _merge_kernel, grid=(nj, R // tm),
        in_specs=[rows(A_WIDTH), rows(HW), rows(HW), gate(0), gate(1), gate(2),
                  wspec(A_WIDTH), wspec(HW), wspec(HW)],
        out_specs=pl.BlockSpec((tm, tn), lambda j, i: (i, j)),
        out_shape=jax.ShapeDtypeStruct((R, D), BF16),
        compiler_params=_cparams("parallel", "parallel"), name="merge",
    )(ya, yb, yc, gates, gates, gates, wa, wb, wc)


def _outproj_kernel(m_ref, w_ref, x_ref, g_ref, x1_ref, h2_ref):
    x1 = x_ref[...] + _dot(m_ref[...], w_ref[...])
    x1_ref[...] = x1
    h2_ref[...] = (_rms_rows(x1) * g_ref[...]).astype(h2_ref.dtype)


def _outproj(m, w_out, x, gains, l):
    R, D = x.shape
    tm = _tile(R, (256, 128, SUBLANES))
    rows = pl.BlockSpec((tm, D), lambda i: (i, 0))
    return pl.pallas_call(
        _outproj_kernel, grid=(R // tm,),
        in_specs=[rows, pl.BlockSpec((None, D, D), lambda i: (l, 0, 0)), rows,
                  pl.BlockSpec((None, 1, D), lambda i: (l, 0, 0))],
        out_specs=[rows, rows],
        out_shape=[jax.ShapeDtypeStruct((R, D), F32), jax.ShapeDtypeStruct((R, D), BF16)],
        compiler_params=_cparams("parallel"), name="outproj")(m, w_out, x, gains)


def _ffn_up_kernel(h_ref, wg_ref, wu_ref, o_ref):
    h = h_ref[...]
    o_ref[...] = (_silu(_dot(h, wg_ref[...])) * _dot(h, wu_ref[...])).astype(o_ref.dtype)


def _ffn_up(h2, w_gate, w_up, l):
    R, D = h2.shape
    F = w_gate.shape[-1]
    tm = _tile(R, (512, 256, 128, SUBLANES))
    tn = _tile(F, (512, 256, LANES))
    wspec = pl.BlockSpec((None, D, tn), lambda j, i: (l, 0, j))
    return pl.pallas_call(
        _ffn_up_kernel, grid=(F // tn, R // tm),
        in_specs=[pl.BlockSpec((tm, D), lambda j, i: (i, 0)), wspec, wspec],
        out_specs=pl.BlockSpec((tm, tn), lambda j, i: (i, j)),
        out_shape=jax.ShapeDtypeStruct((R, F), BF16),
        compiler_params=_cparams("parallel", "parallel"), name="ffn_up")(h2, w_gate, w_up)


def _ffn_down_kernel(a_ref, w_ref, x_ref, o_ref):
    o_ref[...] = x_ref[...] + _dot(a_ref[...], w_ref[...])


def _ffn_down(act, w_down, x1, l):
    R, F = act.shape
    D = x1.shape[-1]
    tm = _tile(R, (512, 256, 128, SUBLANES))
    tn = _tile(D, (512, 256, LANES))
    tile = pl.BlockSpec((tm, tn), lambda j, i: (i, j))
    return pl.pallas_call(
        _ffn_down_kernel, grid=(D // tn, R // tm),
        in_specs=[pl.BlockSpec((tm, F), lambda j, i: (i, 0)),
                  pl.BlockSpec((None, F, tn), lambda j, i: (l, 0, j)), tile],
        out_specs=tile, out_shape=jax.ShapeDtypeStruct((R, D), F32),
        compiler_params=_cparams("parallel", "parallel"), name="ffn_down")(act, w_down, x1)


def _prepare_params(d_model, norm_mix, w_in, a_vnorm, a_ws, a_bs, dn_conv, dn_a_log, dn_dt_bias, dn_onorm,
                    sb_qnorm, sb_knorm, sb_bias, w_branch, w_out, norm_ffn, w_gate, w_up, w_down):
    depth = w_in.shape[0]
    sizes = (A_WIDTH, A_WIDTH, 3 * HW, HW, 2 * HEADS, HW, HW, HW, 3 * d_model)
    names = ("a_u", "a_v", "b_qkv", "b_z", "b_ab", "c_q", "c_k", "c_v", "gates")
    p, lo = {}, 0
    for name, size in zip(names, sizes, strict=True):
        p["w_" + name] = w_in[:, :, lo:lo + size].astype(BF16)
        lo += size
    p["w_b_ab"] = jnp.pad(p["w_b_ab"], ((0, 0), (0, 0), (0, LANES - 2 * HEADS)))
    pad_heads = lambda a: jnp.pad(a, ((0, 0), (0, LANES - HEADS)))[:, None, :]
    tile_heads = lambda a: jnp.tile(a, (1, HEADS))[:, None, :]
    p.update(
        norm_mix=norm_mix[:, None, :], norm_ffn=norm_ffn[:, None, :],
        a_vnorm=a_vnorm.reshape(depth, 1, A_WIDTH), a_ws=a_ws,
        a_bs_rows=jnp.repeat(jnp.swapaxes(a_bs, 1, 2), LANES, axis=2),
        a_ws0=jnp.repeat(a_ws[:, :, 0, 0], LANES, axis=1)[:, None, :],
        a_bs0=jnp.repeat(a_bs[:, :, 0], LANES, axis=1)[:, None, :],
        dn_conv=dn_conv, alog=pad_heads(dn_a_log), dtb=pad_heads(dn_dt_bias),
        dn_onorm=dn_onorm[:, None, :], sb_qnorm=tile_heads(sb_qnorm), sb_knorm=tile_heads(sb_knorm),
        sb_bias_keys=jnp.broadcast_to(sb_bias[:, :, None, None], sb_bias.shape + (1, SB_TK)),
        sb_bias_lanes=jnp.broadcast_to(sb_bias[:, :, None], sb_bias.shape + (LANES,)),
        wb_a=w_branch[:, :A_WIDTH].astype(BF16), wb_b=w_branch[:, A_WIDTH:A_WIDTH + HW].astype(BF16),
        wb_c=w_branch[:, A_WIDTH + HW:].astype(BF16), w_out=w_out.astype(BF16),
        w_gate=w_gate.astype(BF16), w_up=w_up.astype(BF16), w_down=w_down.astype(BF16))
    return p


def _project(x, p, l):
    h = _rms_cast(x, p["norm_mix"], l)
    group_rms = lambda gelu: functools.partial(_epi_group_rms, gelu=gelu)
    return dict(
        gu=_proj(h, p["w_a_u"], l, _epi_gelu),
        vn=_proj(h, p["w_a_v"], l, group_rms(True), p["a_vnorm"]),
        bqkv=_proj(h, p["w_b_qkv"], l, _epi_identity),
        zsilu=_proj(h, p["w_b_z"], l, _epi_silu),
        ab=_proj(h, p["w_b_ab"], l, _epi_identity),
        qc=_proj(h, p["w_c_q"], l, group_rms(False), p["sb_qnorm"]),
        kc=_proj(h, p["w_c_k"], l, group_rms(False), p["sb_knorm"]),
        vc=_proj(h, p["w_c_v"], l, _epi_identity),
        gates=_proj(h, p["w_gates"], l, _epi_sigmoid))


def _finish_layer(x, ya, yb, yc, gates, p, l):
    m = _merge(ya, yb, yc, gates, p["wb_a"], p["wb_b"], p["wb_c"], l)
    x1, h2 = _outproj(m, p["w_out"], x, p["norm_ffn"], l)
    act = _ffn_up(h2, p["w_gate"], p["w_up"], l)
    return _ffn_down(act, p["w_down"], x1, l)


def _prompt_layer(x, p, l, B, T):
    s = _project(x, p, l)
    ya = _mixer_a(s["gu"], s["vn"], p["a_ws"], p["a_bs_rows"], l)
    q, k, v, gb, bb, grow = _gdn_prep(s["bqkv"], s["ab"], p["dn_conv"], p["alog"], p["dtb"], B, T, l)
    yb, state = _gdn(q, k, v, gb, bb, grow, s["zsilu"], p["dn_onorm"], B, T, l)
    yc = _sb_attention(s["qc"], s["kc"], s["vc"], p["sb_bias_keys"], B, T, l)
    y = _finish_layer(x, ya, yb.reshape(B * T, HW), yc.reshape(B * T, HW), s["gates"], p, l)
    conv_tail = s["bqkv"].reshape(B, T, 3 * HW)[:, T - (CONV_W - 1):]
    return y, (s["kc"].reshape(B, T, HEADS, LANES), s["vc"].reshape(B, T, HEADS, LANES), state, conv_tail)


def _sample_layer(x, p, l, cache_k, cache_v, page_table, state_delta, state_conv):
    Bs = x.shape[0]
    s = _project(x, p, l)
    hist = jnp.swapaxes(state_conv, 1, 2)
    ya, yb, state = _sample_mix(s["gu"], s["vn"], p["a_ws0"], p["a_bs0"], s["bqkv"], hist, p["dn_conv"], s["ab"],
                                p["alog"], p["dtb"], s["zsilu"], state_delta, p["dn_onorm"], l)
    yc = _sb_decode(s["qc"], cache_k, cache_v, page_table, p["sb_bias_lanes"], l)
    y = _finish_layer(x, ya, yb, yc.reshape(Bs, HW), s["gates"], p, l)
    conv = jnp.concatenate([state_conv[l][:, 1:], s["bqkv"][:, None, :]], axis=1)
    return y, (s["kc"].reshape(Bs, 1, HEADS, LANES), s["vc"].reshape(Bs, 1, HEADS, LANES), state, conv,
               s["vn"][:, None, :])


def kernel(x_prompt, x_sample, cache_k, cache_v, state_delta, state_conv, page_table, norm_mix, w_in, a_vnorm, a_ws, a_bs, dn_conv, dn_a_log, dn_dt_bias, dn_onorm, sb_qnorm, sb_knorm, sb_bias, w_branch, w_out, norm_ffn, w_gate, w_up, w_down):
    B, T, D = x_prompt.shape
    Bs = x_sample.shape[0]
    depth = w_in.shape[0]
    assert x_sample.shape[1] == 1 and T % SB_TQ == 0 and T % GDN_CHUNK == 0
    p = _prepare_params(D, norm_mix, w_in, a_vnorm, a_ws, a_bs, dn_conv, dn_a_log, dn_dt_bias, dn_onorm,
                        sb_qnorm, sb_knorm, sb_bias, w_branch, w_out, norm_ffn, w_gate, w_up, w_down)
    yp = x_prompt.reshape(B * T, D)
    ys = x_sample.reshape(Bs, D)
    prompt_outs, sample_outs = [], []
    for l in range(depth):
        yp, outs = _prompt_layer(yp, p, l, B, T)
        prompt_outs.append(outs)
        ys, outs = _sample_layer(ys, p, l, cache_k, cache_v, page_table, state_delta, state_conv)
        sample_outs.append(outs)
    stack = lambda outs, i: jnp.stack([o[i] for o in outs])
    return (yp.reshape(B, T, D), ys.reshape(Bs, 1, D),
            stack(prompt_outs, 0), stack(prompt_outs, 1), stack(prompt_outs, 2), stack(prompt_outs, 3),
            stack(sample_outs, 0), stack(sample_outs, 1), stack(sample_outs, 2), stack(sample_outs, 3),
            stack(sample_outs, 4))
```

```python
import functools

import jax
import jax.numpy as jnp
from jax import lax
from jax.experimental import pallas as pl
from jax.experimental.pallas import tpu as pltpu

F32 = jnp.float32
BF16 = jnp.bfloat16
EPS = 1e-6
LANES = 128
SUBLANES = 8
A_GROUPS = 4
A_WIDTH = A_GROUPS * LANES
A_CHUNK = 128
HEADS = 6
HW = HEADS * LANES
CONV_W = 4
GDN_CHUNK = 128
SB_TQ = 256
SB_TK = 256
SB_STREAMS = 3
PAGES_PER_STEP = 8
LOG2E = 1.4426950408889634
VMEM_LIMIT_BYTES = 56 * 1024 * 1024


def _tile(n, candidates):
    for c in candidates:
        if n % c == 0:
            return c
    return n


def _cparams(*sem):
    return pltpu.CompilerParams(dimension_semantics=sem, vmem_limit_bytes=VMEM_LIMIT_BYTES)


def _dot(a, b):
    return jnp.dot(a, b, preferred_element_type=F32)


def _dot_nt(a, b):
    return lax.dot_general(a, b, (((1,), (1,)), ((), ())), preferred_element_type=F32)


def _sigmoid(x):
    return 1.0 / (1.0 + jnp.exp(-x))


def _silu(x):
    return x * _sigmoid(x)


def _softplus(x):
    return jnp.maximum(x, 0.0) + jnp.log(1.0 + jnp.exp(-jnp.abs(x)))


def _rms_rows(x):
    return x * lax.rsqrt(jnp.mean(x * x, axis=-1, keepdims=True) + EPS)


def _l2n_rows(x):
    return x * lax.rsqrt(jnp.sum(x * x, axis=-1, keepdims=True) + EPS)


def _rms_cast_kernel(x_ref, g_ref, o_ref):
    o_ref[...] = (_rms_rows(x_ref[...]) * g_ref[...]).astype(o_ref.dtype)


def _rms_cast(x, gains, l):
    R, D = x.shape
    tr = _tile(R, (512, 256, 128, SUBLANES))
    return pl.pallas_call(
        _rms_cast_kernel, grid=(R // tr,),
        in_specs=[pl.BlockSpec((tr, D), lambda i: (i, 0)),
                  pl.BlockSpec((None, 1, D), lambda i: (l, 0, 0))],
        out_specs=pl.BlockSpec((tr, D), lambda i: (i, 0)),
        out_shape=jax.ShapeDtypeStruct((R, D), BF16),
        compiler_params=_cparams("parallel"), name="rms_cast")(x, gains)


def _epi_identity(acc, o_ref):
    o_ref[...] = acc.astype(o_ref.dtype)


def _epi_gelu(acc, o_ref):
    o_ref[...] = jax.nn.gelu(acc).astype(o_ref.dtype)


def _epi_silu(acc, o_ref):
    o_ref[...] = _silu(acc).astype(o_ref.dtype)


def _epi_sigmoid(acc, o_ref):
    o_ref[...] = _sigmoid(acc).astype(o_ref.dtype)


def _store_group(o_ref, c, val, head_major):
    if head_major:
        o_ref[c] = val.astype(o_ref.dtype)
    else:
        o_ref[:, c * LANES:(c + 1) * LANES] = val.astype(o_ref.dtype)


def _epi_heads(acc, o_ref, *, head_major):
    for c in range(acc.shape[1] // LANES):
        _store_group(o_ref, c, acc[:, c * LANES:(c + 1) * LANES], head_major)


def _epi_group_rms(acc, gain_ref, o_ref, *, gelu, head_major=False):
    for c in range(acc.shape[1] // LANES):
        cs = slice(c * LANES, (c + 1) * LANES)
        blk = acc[:, cs]
        if gelu:
            blk = jax.nn.gelu(blk)
        _store_group(o_ref, c, _rms_rows(blk) * gain_ref[:, cs], head_major)


def _proj_kernel(h_ref, w_ref, *rest, epilogue):
    epilogue(_dot(h_ref[...], w_ref[...]), *rest)


def _proj(h, w, l, epilogue, gain=None, out_dtype=F32, head_major=None):
    R, K = h.shape
    N = w.shape[-1]
    tm = _tile(R, (512, 256, 128, SUBLANES))
    tn = N if head_major else _tile(N, (1024, 768, 512, 384, 256, LANES))
    in_specs = [pl.BlockSpec((tm, K), lambda j, i: (i, 0)),
                pl.BlockSpec((None, K, tn), lambda j, i: (l, 0, j))]
    args = [h, w]
    if gain is not None:
        in_specs.append(pl.BlockSpec((None, 1, tn), lambda j, i: (l, 0, j)))
        args.append(gain)
    if head_major:
        B, T = head_major
        nb = T // tm
        out_spec = pl.BlockSpec((None, N // LANES, tm, LANES), lambda j, i: (i // nb, 0, i % nb, 0))
        out_shape = jax.ShapeDtypeStruct((B, N // LANES, T, LANES), out_dtype)
    else:
        out_spec = pl.BlockSpec((tm, tn), lambda j, i: (i, j))
        out_shape = jax.ShapeDtypeStruct((R, N), out_dtype)
    return pl.pallas_call(
        functools.partial(_proj_kernel, epilogue=epilogue), grid=(N // tn, R // tm),
        in_specs=in_specs, out_specs=out_spec, out_shape=out_shape,
        compiler_params=_cparams("parallel", "parallel"), name="proj")(*args)


def _mixer_a_kernel(gu_ref, vn_ref, ws_ref, bs_ref, o_ref, *, chunks):
    t = lax.broadcasted_iota(jnp.int32, (A_CHUNK, A_CHUNK), 0)
    s = lax.broadcasted_iota(jnp.int32, (A_CHUNK, A_CHUNK), 1)
    for g in range(A_GROUPS):
        cs = slice(g * LANES, (g + 1) * LANES)
        w = jnp.where(t >= s, ws_ref[g], 0.0).astype(BF16)
        for c in range(chunks):
            rs = slice(c * A_CHUNK, (c + 1) * A_CHUNK)
            mixed = _dot(w, vn_ref[rs, cs].astype(BF16)) + bs_ref[:, cs]
            o_ref[rs, cs] = (gu_ref[rs, cs] * mixed).astype(o_ref.dtype)


def _mixer_a(gu, vn, a_ws, bs_rows, l):
    R = gu.shape[0]
    chunks = _tile(R // A_CHUNK, (4, 2, 1))
    tr = chunks * A_CHUNK
    return pl.pallas_call(
        functools.partial(_mixer_a_kernel, chunks=chunks), grid=(R // tr,),
        in_specs=[pl.BlockSpec((tr, A_WIDTH), lambda i: (i, 0)),
                  pl.BlockSpec((tr, A_WIDTH), lambda i: (i, 0)),
                  pl.BlockSpec((None, A_GROUPS, A_CHUNK, A_CHUNK), lambda i: (l, 0, 0, 0)),
                  pl.BlockSpec((None, A_CHUNK, A_WIDTH), lambda i: (l, 0, 0))],
        out_specs=pl.BlockSpec((tr, A_WIDTH), lambda i: (i, 0)),
        out_shape=jax.ShapeDtypeStruct((R, A_WIDTH), F32),
        compiler_params=_cparams("parallel"), name="mixer_a")(gu, vn, a_ws, bs_rows)


def _split3_bf16(x):
    a = x.astype(BF16)
    r = x - a.astype(F32)
    b = r.astype(BF16)
    c = (r - b.astype(F32)).astype(BF16)
    return a, b, c


def _gdn_prep_kernel(x_ref, halo_ref, cw_ref, ab_ref, alog_ref, dtb_ref,
                     q_ref, k_ref, v_ref, gb_ref, bb_ref, grow_ref, xp_ref, *, tr):
    i = pl.program_id(1)
    xp_ref[0:SUBLANES, :] = jnp.where(i == 0, 0.0, halo_ref[...])
    xp_ref[SUBLANES:, :] = x_ref[...]
    y = cw_ref[CONV_W - 1:CONV_W, :] * x_ref[...]
    for s in range(1, CONV_W):
        y = y + cw_ref[CONV_W - 1 - s:CONV_W - s, :] * xp_ref[pl.ds(SUBLANES - s, tr), :]
    y = _silu(y)
    for h in range(HEADS):
        hs = slice(h * LANES, (h + 1) * LANES)
        q_ref[:, hs] = _l2n_rows(y[:, hs]) * (LANES ** -0.5)
        k_ref[:, hs] = _l2n_rows(y[:, HW + h * LANES:HW + (h + 1) * LANES])
    v_ref[...] = y[:, 2 * HW:]
    ab = ab_ref[...]
    g = -jnp.exp(alog_ref[...]) * _softplus(ab + dtb_ref[...])
    beta = _sigmoid(ab)
    r = lax.broadcasted_iota(jnp.int32, (tr, tr), 0)
    c = lax.broadcasted_iota(jnp.int32, (tr, tr), 1)
    tri = jnp.where((r >= c) & (r // GDN_CHUNK == c // GDN_CHUNK), 1.0, 0.0).astype(BF16)
    gcum = sum(_dot(tri, part) for part in _split3_bf16(g))
    for h in range(HEADS):
        hs = slice(h * LANES, (h + 1) * LANES)
        gb_ref[:, hs] = jnp.broadcast_to(gcum[:, h:h + 1], (tr, LANES))
        bb_ref[:, hs] = jnp.broadcast_to(beta[:, HEADS + h:HEADS + h + 1], (tr, LANES))
    gt = gcum.T
    for h in range(HEADS):
        grow_ref[h] = gt[h:h + 1, :]


def _gdn_prep(bqkv, ab, conv_w, alog_rows, dtb_rows, B, T, l):
    tr = _tile(T, (256, GDN_CHUNK))
    nb = T // tr
    row = lambda b, i: (b * nb + i, 0)
    halo = lambda b, i: (jnp.maximum((b * T + i * tr) // SUBLANES - 1, 0), 0)
    par = lambda b, i: (l, 0, 0)
    R = B * T
    wide = jax.ShapeDtypeStruct((R, HW), F32)
    return pl.pallas_call(
        functools.partial(_gdn_prep_kernel, tr=tr), grid=(B, nb),
        in_specs=[pl.BlockSpec((tr, 3 * HW), row),
                  pl.BlockSpec((SUBLANES, 3 * HW), halo),
                  pl.BlockSpec((None, CONV_W, 3 * HW), par),
                  pl.BlockSpec((tr, LANES), row),
                  pl.BlockSpec((None, 1, LANES), par),
                  pl.BlockSpec((None, 1, LANES), par)],
        out_specs=[pl.BlockSpec((tr, HW), row)] * 5
        + [pl.BlockSpec((None, HEADS, 1, tr), lambda b, i: (b, 0, 0, i))],
        out_shape=[wide] * 5 + [jax.ShapeDtypeStruct((B, HEADS, 1, T), F32)],
        scratch_shapes=[pltpu.VMEM((tr + SUBLANES, 3 * HW), F32)],
        compiler_params=_cparams("parallel", "parallel"), name="gdn_prep",
    )(bqkv, bqkv, conv_w, ab, alog_rows, dtb_rows)


INV_BASE = 16


def _unit_lower_inverses_minus_eye(ns, r, c):
    L = ns[0].shape[0]
    shift = INV_BASE.bit_length() - 1
    same_block = (r >> shift) == (c >> shift)
    xos = [jnp.where(same_block, n, 0.0) for n in ns]
    nbs = [x.astype(BF16) for x in xos]
    ps = [_dot(nb, nb) for nb in nbs]
    for _ in range(shift - 2):
        boths = [_dot(jnp.concatenate([xo, p], axis=0).astype(BF16), p.astype(BF16)) for xo, p in zip(xos, ps)]
        xos = [xo + p + both[:L] for xo, p, both in zip(xos, ps, boths)]
        ps = [both[L:] for both in boths]
    xps = [_dot(xo.astype(BF16), p.astype(BF16)) for xo, p in zip(xos, ps)]
    xos = [xo + p + xp for xo, p, xp in zip(xos, ps, xps)]
    while (1 << shift) < L:
        below_left = ((r >> shift) ^ (c >> shift)) == 1
        offs = [jnp.where(below_left, n, 0.0) for n in ns]
        xbs = [xo.astype(BF16) for xo in xos]
        ys = [off + _dot(xb, off.astype(BF16)) for off, xb in zip(offs, xbs)]
        yxs = [_dot(y.astype(BF16), xb) for y, xb in zip(ys, xbs)]
        xos = [xo + y + yx for xo, y, yx in zip(xos, ys, yxs)]
        shift += 1
    return xos


def _gdn_chunk_updates(qs, ks, vs, gcs, bts, grows, states):
    L = qs[0].shape[0]
    r = lax.broadcasted_iota(jnp.int32, (L, L), 0)
    c = lax.broadcasted_iota(jnp.int32, (L, L), 1)
    decays = [jnp.where(r >= c, jnp.exp(gc - grow), 0.0) for gc, grow in zip(gcs, grows)]
    kqs = [_dot_nt(jnp.concatenate([k, q], axis=0).astype(BF16), k.astype(BF16)) for k, q in zip(ks, qs)]
    ns = [jnp.where(r > c, -(bt * kq[:L] * decay), 0.0) for bt, kq, decay in zip(bts, kqs, decays)]
    xos = _unit_lower_inverses_minus_eye(ns, r, c)
    gams = [jnp.exp(gc) for gc in gcs]
    bvs = [bt * v for bt, v in zip(bts, vs)]
    bgks = [bt * gam * k for bt, gam, k in zip(bts, gams, ks)]
    corrs = [_dot(xo.astype(BF16), jnp.concatenate([bv, bgk], axis=1).astype(BF16))
             for xo, bv, bgk in zip(xos, bvs, bgks)]
    wqs = [_dot(jnp.concatenate([bgk + corr[:, LANES:], gam * q], axis=0).astype(BF16), state.astype(BF16))
           for bgk, corr, gam, q, state in zip(bgks, corrs, gams, qs, states)]
    us = [(bv + corr[:, :LANES] - wq[:L]).astype(BF16) for bv, corr, wq in zip(bvs, corrs, wqs)]
    glasts = [gc[L - 1:L, :] for gc in gcs]
    kds = [(k * jnp.exp(glast - gc)).astype(BF16) for k, glast, gc in zip(ks, glasts, gcs)]
    outs = [wq[L:] + _dot((kq[L:] * decay).astype(BF16), u) for wq, kq, decay, u in zip(wqs, kqs, decays, us)]
    new_states = [jnp.exp(glast) * state + lax.dot_general(kd, u, (((0,), (0,)), ((), ())),
                                                           preferred_element_type=F32)
                  for glast, state, kd, u in zip(glasts, states, kds, us)]
    return outs, new_states


def _gdn_kernel(q_ref, k_ref, v_ref, gb_ref, bb_ref, grow_ref, z_ref, go_ref, y_ref, s_ref, *, B):
    @pl.when(pl.program_id(0) == 0)
    def _():
        s_ref[...] = jnp.zeros_like(s_ref)

    units = [(b, h, slice(h * LANES, (h + 1) * LANES)) for b in range(B) for h in range(HEADS)]
    outs, new_states = _gdn_chunk_updates(
        [q_ref[b, :, hs] for b, h, hs in units], [k_ref[b, :, hs] for b, h, hs in units],
        [v_ref[b, :, hs] for b, h, hs in units], [gb_ref[b, :, hs] for b, h, hs in units],
        [bb_ref[b, :, hs] for b, h, hs in units], [grow_ref[b, h] for b, h, hs in units],
        [s_ref[b, h] for b, h, hs in units])
    for (b, h, hs), out, new_state in zip(units, outs, new_states):
        s_ref[b, h] = new_state
        y_ref[b, :, hs] = _rms_rows(out) * go_ref[...] * z_ref[b, :, hs]


def _gdn(q, k, v, gb, bb, grow, zsilu, go_rows, B, T, l):
    L = GDN_CHUNK
    blk = pl.BlockSpec((B, L, HW), lambda i: (0, i, 0))
    three = lambda a: a.reshape(B, T, HW)
    return pl.pallas_call(
        functools.partial(_gdn_kernel, B=B), grid=(T // L,),
        in_specs=[blk] * 5 + [pl.BlockSpec((B, HEADS, 1, L), lambda i: (0, 0, 0, i)), blk,
                              pl.BlockSpec((None, 1, LANES), lambda i: (l, 0, 0))],
        out_specs=[blk, pl.BlockSpec((B, HEADS, LANES, LANES), lambda i: (0, 0, 0, 0))],
        out_shape=[jax.ShapeDtypeStruct((B, T, HW), F32),
                   jax.ShapeDtypeStruct((B, HEADS, LANES, LANES), F32)],
        compiler_params=_cparams("arbitrary"), name="gdn",
    )(three(q), three(k), three(v), three(gb), three(bb), grow, three(zsilu), go_rows)


def _softplus2(z):
    return jnp.maximum(z, 0.0) + jnp.log2(1.0 + jnp.exp2(-jnp.abs(z)))


def _sb_blocks(qs, k_ref, v_ref, biases, jj, carries, accs, *, diagonal):
    tk = SB_TK
    ks = pl.ds(pl.multiple_of(jj * tk, tk), tk)
    r = lax.broadcasted_iota(jnp.int32, (tk, tk), 0)
    s = lax.broadcasted_iota(jnp.int32, (tk, tk), 1)
    later = jnp.where(r > s, 1.0, 0.0).astype(BF16)
    mask = s < r
    zs = [_dot_nt(q, k_ref[n, ks, :].astype(BF16)) + bias for n, (q, bias) in enumerate(zip(qs, biases))]
    sps = [_softplus2(z) for z in zs]
    if diagonal:
        sps = [jnp.where(mask, sp, 0.0) for sp in sps]
    tails = [_dot(sp.astype(BF16), later) for sp in sps]
    ws = [jnp.exp2((z - sp) - tail - carry) for z, sp, tail, carry in zip(zs, sps, tails, carries)]
    if diagonal:
        ws = [jnp.where(mask, w, 0.0) for w in ws]
    accs = [acc + _dot(w.astype(BF16), v_ref[n, ks, :].astype(BF16)) for n, (acc, w) in enumerate(zip(accs, ws))]
    carries = [carry + (tail[:, :1] + sp[:, :1]) for carry, tail, sp in zip(carries, tails, sps)]
    return tuple(carries), tuple(accs)


def _sb_kernel(q_ref, k_ref, v_ref, bias_ref, o_ref):
    i = pl.program_id(2)
    streams = range(SB_STREAMS)
    qs = [(q_ref[:, n * LANES:(n + 1) * LANES] * (LANES ** -0.5 * LOG2E)).astype(BF16) for n in streams]
    biases = [bias_ref[n] for n in streams]
    carries = tuple(jnp.zeros((SB_TQ, 1), F32) for _ in streams)
    accs = tuple(jnp.zeros((SB_TQ, LANES), F32) for _ in streams)
    state = _sb_blocks(qs, k_ref, v_ref, biases, i, carries, accs, diagonal=True)

    def body(step, state):
        return _sb_blocks(qs, k_ref, v_ref, biases, i - 1 - step, *state, diagonal=False)

    carries, accs = lax.fori_loop(0, i, body, state)
    for n in streams:
        o_ref[:, n * LANES:(n + 1) * LANES] = accs[n]


def _sb_attention(qc, kc, vc, bias2_keys, B, T, l):
    assert SB_TQ == SB_TK and T % SB_TQ == 0 and HEADS % SB_STREAMS == 0
    width = SB_STREAMS * LANES
    kv = pl.BlockSpec((None, SB_STREAMS, T, LANES), lambda b, g, i: (b, g, 0, 0))
    qo = pl.BlockSpec((None, SB_TQ, width), lambda b, g, i: (b, i, g))
    return pl.pallas_call(
        _sb_kernel, grid=(B, HEADS // SB_STREAMS, T // SB_TQ),
        in_specs=[qo, kv, kv, pl.BlockSpec((None, SB_STREAMS, 1, SB_TK), lambda b, g, i: (l, g, 0, 0))],
        out_specs=qo, out_shape=jax.ShapeDtypeStruct((B, T, HW), F32),
        compiler_params=_cparams("parallel", "parallel", "parallel"), name="sb_attention",
    )(qc.reshape(B, T, HW), kc, vc, bias2_keys)


def _sb_decode_kernel(pt_ref, q_ref, bias_ref, *rest):
    del pt_ref
    k_refs = rest[:PAGES_PER_STEP]
    v_refs = rest[PAGES_PER_STEP:2 * PAGES_PER_STEP]
    o_ref, carry_ref = rest[2 * PAGES_PER_STEP:]
    j = pl.program_id(1)

    @pl.when(j == 0)
    def _():
        o_ref[...] = jnp.zeros_like(o_ref)
        carry_ref[...] = jnp.zeros_like(carry_ref)

    qb = (q_ref[...] * (LANES ** -0.5 * LOG2E)).astype(BF16)
    bias = bias_ref[...]
    P = k_refs[0].shape[1]
    head = lax.broadcasted_iota(jnp.int32, (SUBLANES, LANES), 0)
    r = lax.broadcasted_iota(jnp.int32, (P, P), 0)
    s = lax.broadcasted_iota(jnp.int32, (P, P), 1)
    later = jnp.where(r > s, 1.0, 0.0).astype(BF16)
    pages = range(PAGES_PER_STEP)
    zhs = [[_dot_nt(qb, k_refs[p][h].astype(BF16)) for h in range(HEADS)] for p in pages]
    zs = []
    for p in pages:
        z = zhs[p][0]
        for h in range(1, HEADS):
            z = jnp.where(head == h, zhs[p][h], z)
        zs.append(z + bias)
    sps = [_softplus2(z) for z in zs]
    tails = [_dot(sp.astype(BF16), later) for sp in sps]
    carries = [carry_ref[...]]
    for p in pages:
        carries.append(carries[-1] + (tails[p][:, :1] + sps[p][:, :1]))
    ws = [jnp.exp2((zs[p] - sps[p]) - tails[p] - carries[p]).astype(BF16) for p in pages]
    ohs = [[_dot(ws[p], v_refs[p][h].astype(BF16)) for h in range(HEADS)] for p in pages]
    acc = o_ref[...]
    for h in range(HEADS):
        acc = acc + jnp.where(head == h, sum(ohs[p][h] for p in pages), 0.0)
    o_ref[...] = acc
    carry_ref[...] = carries[-1]


def _sb_decode(qc, cache_k, cache_v, page_table, bias2_lanes, l):
    Bs, NP = page_table.shape
    P = cache_k.shape[3]
    assert NP % PAGES_PER_STEP == 0 and P == LANES
    pt = page_table.reshape(-1)
    q = jnp.pad(qc.reshape(Bs, HEADS, LANES), ((0, 0), (0, SUBLANES - HEADS), (0, 0)))

    def page_spec(p):
        def index(b, j, pt_ref):
            return (l, pt_ref[b * NP + (NP - 1 - (j * PAGES_PER_STEP + p))], 0, 0, 0)
        return pl.BlockSpec((None, None, HEADS, P, LANES), index)

    row = pl.BlockSpec((None, SUBLANES, LANES), lambda b, j, pt_ref: (b, 0, 0))
    pages = [page_spec(p) for p in range(PAGES_PER_STEP)]
    out = pl.pallas_call(
        _sb_decode_kernel,
        grid_spec=pltpu.PrefetchScalarGridSpec(
            num_scalar_prefetch=1, grid=(Bs, NP // PAGES_PER_STEP),
            in_specs=[row, pl.BlockSpec((None, SUBLANES, LANES), lambda b, j, pt_ref: (l, 0, 0))] + pages + pages,
            out_specs=row, scratch_shapes=[pltpu.VMEM((SUBLANES, LANES), F32)]),
        out_shape=jax.ShapeDtypeStruct((Bs, SUBLANES, LANES), F32),
        compiler_params=_cparams("parallel", "arbitrary"), name="sb_decode",
    )(pt, q, bias2_lanes, *([cache_k] * PAGES_PER_STEP), *([cache_v] * PAGES_PER_STEP))
    return out[:, :HEADS].reshape(Bs, HW)


def _columns(x):
    pad = jnp.zeros((LANES - x.shape[0], LANES), F32)
    return jnp.concatenate([x, pad], axis=0).T


def _sample_mix_kernel(gu_ref, vn_ref, ws_ref, bs_ref, x_ref, hist_ref, cw_ref, ab_ref, alog_ref, dtb_ref,
                       z_ref, s_ref, go_ref, ya_ref, yb_ref, sn_ref, o_scr):
    Bs = gu_ref.shape[0]
    ya_ref[...] = gu_ref[...] * (ws_ref[...] * vn_ref[...] + bs_ref[...])
    y = cw_ref[CONV_W - 1:CONV_W, :] * x_ref[...]
    for s in range(CONV_W - 1):
        y = y + cw_ref[s:s + 1, :] * hist_ref[s]
    y = _silu(y)
    ab = ab_ref[...]
    decay = jnp.exp(-jnp.exp(alog_ref[...]) * _softplus(ab + dtb_ref[...]))
    beta = _sigmoid(ab)
    for h in range(HEADS):
        hs = slice(h * LANES, (h + 1) * LANES)
        q = _l2n_rows(y[:, hs]) * (LANES ** -0.5)
        k = _l2n_rows(y[:, HW + h * LANES:HW + (h + 1) * LANES])
        v = y[:, 2 * HW + h * LANES:2 * HW + (h + 1) * LANES]
        qcols, kcols = _columns(q), _columns(k)
        for b in range(Bs):
            a = decay[b:b + 1, h:h + 1]
            bt = beta[b:b + 1, HEADS + h:HEADS + h + 1]
            state = s_ref[b, h]
            kcol = kcols[:, b:b + 1]
            ks = jnp.sum(kcol * state, axis=0, keepdims=True)
            u = bt * (v[b:b + 1, :] - a * ks)
            new_state = a * state + kcol * u
            sn_ref[b, h] = new_state
            o_scr[b:b + 1, hs] = jnp.sum(qcols[:, b:b + 1] * new_state, axis=0, keepdims=True)
    for h in range(HEADS):
        hs = slice(h * LANES, (h + 1) * LANES)
        yb_ref[:, hs] = _rms_rows(o_scr[:, hs]) * go_ref[...] * z_ref[:, hs]


def _sample_mix(gu, vn, ws_rows, bs_rows, bqkv, hist, conv_w, ab, alog_rows, dtb_rows, zsilu, state, go_rows, l):
    Bs = gu.shape[0]
    full = lambda shape: pl.BlockSpec(shape, lambda i: (0,) * len(shape))
    par = lambda shape: pl.BlockSpec((None,) + shape, lambda i: (l,) + (0,) * len(shape))
    return pl.pallas_call(
        _sample_mix_kernel, grid=(1,),
        in_specs=[full((Bs, A_WIDTH)), full((Bs, A_WIDTH)), par((1, A_WIDTH)), par((1, A_WIDTH)),
                  full((Bs, 3 * HW)), par((CONV_W - 1, Bs, 3 * HW)), par((CONV_W, 3 * HW)),
                  full((Bs, LANES)), par((1, LANES)), par((1, LANES)), full((Bs, HW)),
                  par((Bs, HEADS, LANES, LANES)), par((1, LANES))],
        out_specs=[full((Bs, A_WIDTH)), full((Bs, HW)), full((Bs, HEADS, LANES, LANES))],
        out_shape=[jax.ShapeDtypeStruct((Bs, A_WIDTH), F32), jax.ShapeDtypeStruct((Bs, HW), F32),
                   jax.ShapeDtypeStruct((Bs, HEADS, LANES, LANES), F32)],
        scratch_shapes=[pltpu.VMEM((Bs, HW), F32)],
        compiler_params=_cparams("arbitrary"), name="sample_mix",
    )(gu, vn, ws_rows, bs_rows, bqkv, hist, conv_w, ab, alog_rows, dtb_rows, zsilu, state, go_rows)


def _merge_kernel(ya_ref, yb_ref, yc_ref, ga_ref, gb_ref, gc_ref, wa_ref, wb_ref, wc_ref, o_ref):
    m = (ga_ref[...] * _dot(ya_ref[...].astype(BF16), wa_ref[...])
         + gb_ref[...] * _dot(yb_ref[...].astype(BF16), wb_ref[...])
         + gc_ref[...] * _dot(yc_ref[...].astype(BF16), wc_ref[...]))
    o_ref[...] = m.astype(o_ref.dtype)


def _merge(ya, yb, yc, gates, wa, wb, wc, l):
    R = ya.shape[0]
    D = wa.shape[-1]
    tm = _tile(R, (512, 256, 128, SUBLANES))
    tn = _tile(D, (1024, 512, 256, LANES))
    nj = D // tn
    rows = lambda width: pl.BlockSpec((tm, width), lambda j, i: (i, 0))
    gate = lambda k: pl.BlockSpec((tm, tn), lambda j, i: (i, k * nj + j))
    wspec = lambda width: pl.BlockSpec((None, width, tn), lambda j, i: (l, 0, j))
    return pl.pallas_call(
        _merge_kernel, grid=(nj, R // tm),
        in_specs=[rows(A_WIDTH), rows(HW), rows(HW), gate(0), gate(1), gate(2),
                  wspec(A_WIDTH), wspec(HW), wspec(HW)],
        out_specs=pl.BlockSpec((tm, tn), lambda j, i: (i, j)),
        out_shape=jax.ShapeDtypeStruct((R, D), BF16),
        compiler_params=_cparams("parallel", "parallel"), name="merge",
    )(ya, yb, yc, gates, gates, gates, wa, wb, wc)


def _outproj_kernel(m_ref, w_ref, x_ref, g_ref, x1_ref, h2_ref):
    x1 = x_ref[...] + _dot(m_ref[...], w_ref[...])
    x1_ref[...] = x1
    h2_ref[...] = (_rms_rows(x1) * g_ref[...]).astype(h2_ref.dtype)


def _outproj(m, w_out, x, gains, l):
    R, D = x.shape
    tm = _tile(R, (256, 128, SUBLANES))
    rows = pl.BlockSpec((tm, D), lambda i: (i, 0))
    return pl.pallas_call(
        _outproj_kernel, grid=(R // tm,),
        in_specs=[rows, pl.BlockSpec((None, D, D), lambda i: (l, 0, 0)), rows,
                  pl.BlockSpec((None, 1, D), lambda i: (l, 0, 0))],
        out_specs=[rows, rows],
        out_shape=[jax.ShapeDtypeStruct((R, D), F32), jax.ShapeDtypeStruct((R, D), BF16)],
        compiler_params=_cparams("parallel"), name="outproj")(m, w_out, x, gains)


def _ffn_up_kernel(h_ref, wg_ref, wu_ref, o_ref):
    h = h_ref[...]
    o_ref[...] = (_silu(_dot(h, wg_ref[...])) * _dot(h, wu_ref[...])).astype(o_ref.dtype)


def _ffn_up(h2, w_gate, w_up, l):
    R, D = h2.shape
    F = w_gate.shape[-1]
    tm = _tile(R, (512, 256, 128, SUBLANES))
    tn = _tile(F, (512, 256, LANES))
    wspec = pl.BlockSpec((None, D, tn), lambda j, i: (l, 0, j))
    return pl.pallas_call(
        _ffn_up_kernel, grid=(F // tn, R // tm),
        in_specs=[pl.BlockSpec((tm, D), lambda j, i: (i, 0)), wspec, wspec],
        out_specs=pl.BlockSpec((tm, tn), lambda j, i: (i, j)),
        out_shape=jax.ShapeDtypeStruct((R, F), BF16),
        compiler_params=_cparams("parallel", "parallel"), name="ffn_up")(h2, w_gate, w_up)


def _ffn_down_kernel(a_ref, w_ref, x_ref, o_ref):
    o_ref[...] = x_ref[...] + _dot(a_ref[...], w_ref[...])


def _ffn_down(act, w_down, x1, l):
    R, F = act.shape
    D = x1.shape[-1]
    tm = _tile(R, (512, 256, 128, SUBLANES))
    tn = _tile(D, (512, 256, LANES))
    tile = pl.BlockSpec((tm, tn), lambda j, i: (i, j))
    return pl.pallas_call(
        _ffn_down_kernel, grid=(D // tn, R // tm),
        in_specs=[pl.BlockSpec((tm, F), lambda j, i: (i, 0)),
                  pl.BlockSpec((None, F, tn), lambda j, i: (l, 0, j)), tile],
        out_specs=tile, out_shape=jax.ShapeDtypeStruct((R, D), F32),
        compiler_params=_cparams("parallel", "parallel"), name="ffn_down")(act, w_down, x1)


def _prepare_params(d_model, norm_mix, w_in, a_vnorm, a_ws, a_bs, dn_conv, dn_a_log, dn_dt_bias, dn_onorm,
                    sb_qnorm, sb_knorm, sb_bias, w_branch, w_out, norm_ffn, w_gate, w_up, w_down):
    depth = w_in.shape[0]
    sizes = (A_WIDTH, A_WIDTH, 3 * HW, HW, 2 * HEADS, HW, HW, HW, 3 * d_model)
    names = ("a_u", "a_v", "b_qkv", "b_z", "b_ab", "c_q", "c_k", "c_v", "gates")
    p, lo = {}, 0
    for name, size in zip(names, sizes, strict=True):
        p["w_" + name] = w_in[:, :, lo:lo + size].astype(BF16)
        lo += size
    p["w_b_ab"] = jnp.pad(p["w_b_ab"], ((0, 0), (0, 0), (0, LANES - 2 * HEADS)))
    pad_heads = lambda a: jnp.pad(a, ((0, 0), (0, LANES - HEADS)))[:, None, :]
    tile_heads = lambda a: jnp.tile(a, (1, HEADS))[:, None, :]
    p.update(
        norm_mix=norm_mix[:, None, :], norm_ffn=norm_ffn[:, None, :],
        a_vnorm=a_vnorm.reshape(depth, 1, A_WIDTH), a_ws=a_ws,
        a_bs_rows=jnp.repeat(jnp.swapaxes(a_bs, 1, 2), LANES, axis=2),
        a_ws0=jnp.repeat(a_ws[:, :, 0, 0], LANES, axis=1)[:, None, :],
        a_bs0=jnp.repeat(a_bs[:, :, 0], LANES, axis=1)[:, None, :],
        dn_conv=dn_conv, alog=pad_heads(dn_a_log), dtb=pad_heads(dn_dt_bias),
        dn_onorm=dn_onorm[:, None, :], sb_qnorm=tile_heads(sb_qnorm), sb_knorm=tile_heads(sb_knorm),
        sb_bias2_keys=jnp.broadcast_to((sb_bias * LOG2E)[:, :, None, None], sb_bias.shape + (1, SB_TK)),
        sb_bias2_lanes=jnp.broadcast_to(jnp.pad(sb_bias * LOG2E, ((0, 0), (0, SUBLANES - HEADS)))[:, :, None],
                                        (depth, SUBLANES, LANES)),
        wb_a=w_branch[:, :A_WIDTH].astype(BF16), wb_b=w_branch[:, A_WIDTH:A_WIDTH + HW].astype(BF16),
        wb_c=w_branch[:, A_WIDTH + HW:].astype(BF16), w_out=w_out.astype(BF16),
        w_gate=w_gate.astype(BF16), w_up=w_up.astype(BF16), w_down=w_down.astype(BF16))
    return p


def _project(x, p, l, head_major=None):
    h = _rms_cast(x, p["norm_mix"], l)
    group_rms = lambda gelu, hm=False: functools.partial(_epi_group_rms, gelu=gelu, head_major=hm)
    hm = bool(head_major)
    return dict(
        gu=_proj(h, p["w_a_u"], l, _epi_gelu),
        vn=_proj(h, p["w_a_v"], l, group_rms(True), p["a_vnorm"]),
        bqkv=_proj(h, p["w_b_qkv"], l, _epi_identity),
        zsilu=_proj(h, p["w_b_z"], l, _epi_silu),
        ab=_proj(h, p["w_b_ab"], l, _epi_identity),
        qc=_proj(h, p["w_c_q"], l, group_rms(False), p["sb_qnorm"]),
        kc=_proj(h, p["w_c_k"], l, group_rms(False, hm), p["sb_knorm"], head_major=head_major),
        vc=_proj(h, p["w_c_v"], l, functools.partial(_epi_heads, head_major=hm), head_major=head_major),
        gates=_proj(h, p["w_gates"], l, _epi_sigmoid))


def _finish_layer(x, ya, yb, yc, gates, p, l):
    m = _merge(ya, yb, yc, gates, p["wb_a"], p["wb_b"], p["wb_c"], l)
    x1, h2 = _outproj(m, p["w_out"], x, p["norm_ffn"], l)
    act = _ffn_up(h2, p["w_gate"], p["w_up"], l)
    return _ffn_down(act, p["w_down"], x1, l)


def _prompt_layer(x, p, l, B, T):
    s = _project(x, p, l, head_major=(B, T))
    ya = _mixer_a(s["gu"], s["vn"], p["a_ws"], p["a_bs_rows"], l)
    q, k, v, gb, bb, grow = _gdn_prep(s["bqkv"], s["ab"], p["dn_conv"], p["alog"], p["dtb"], B, T, l)
    yb, state = _gdn(q, k, v, gb, bb, grow, s["zsilu"], p["dn_onorm"], B, T, l)
    yc = _sb_attention(s["qc"], s["kc"], s["vc"], p["sb_bias2_keys"], B, T, l)
    y = _finish_layer(x, ya, yb.reshape(B * T, HW), yc.reshape(B * T, HW), s["gates"], p, l)
    conv_tail = s["bqkv"].reshape(B, T, 3 * HW)[:, T - (CONV_W - 1):]
    return y, (s["kc"], s["vc"], state, conv_tail)


def _sample_layer(x, p, l, cache_k, cache_v, page_table, state_delta, state_conv):
    Bs = x.shape[0]
    s = _project(x, p, l)
    hist = jnp.swapaxes(state_conv, 1, 2)
    ya, yb, state = _sample_mix(s["gu"], s["vn"], p["a_ws0"], p["a_bs0"], s["bqkv"], hist, p["dn_conv"], s["ab"],
                                p["alog"], p["dtb"], s["zsilu"], state_delta, p["dn_onorm"], l)
    yc = _sb_decode(s["qc"], cache_k, cache_v, page_table, p["sb_bias2_lanes"], l)
    y = _finish_layer(x, ya, yb, yc, s["gates"], p, l)
    conv = jnp.concatenate([state_conv[l][:, 1:], s["bqkv"][:, None, :]], axis=1)
    return y, (s["kc"].reshape(Bs, 1, HEADS, LANES), s["vc"].reshape(Bs, 1, HEADS, LANES), state, conv,
               s["vn"][:, None, :])


def kernel(x_prompt, x_sample, cache_k, cache_v, state_delta, state_conv, page_table, norm_mix, w_in, a_vnorm, a_ws, a_bs, dn_conv, dn_a_log, dn_dt_bias, dn_onorm, sb_qnorm, sb_knorm, sb_bias, w_branch, w_out, norm_ffn, w_gate, w_up, w_down):
    B, T, D = x_prompt.shape
    Bs = x_sample.shape[0]
    depth = w_in.shape[0]
    assert x_sample.shape[1] == 1 and T % SB_TQ == 0 and T % GDN_CHUNK == 0
    p = _prepare_params(D, norm_mix, w_in, a_vnorm, a_ws, a_bs, dn_conv, dn_a_log, dn_dt_bias, dn_onorm,
                        sb_qnorm, sb_knorm, sb_bias, w_branch, w_out, norm_ffn, w_gate, w_up, w_down)
    cache_k = jnp.swapaxes(cache_k, 2, 3)
    cache_v = jnp.swapaxes(cache_v, 2, 3)
    yp = x_prompt.reshape(B * T, D)
    ys = x_sample.reshape(Bs, D)
    prompt_outs, sample_outs = [], []
    for l in range(depth):
        yp, outs = _prompt_layer(yp, p, l, B, T)
        prompt_outs.append(outs)
        ys, outs = _sample_layer(ys, p, l, cache_k, cache_v, page_table, state_delta, state_conv)
        sample_outs.append(outs)
    stack = lambda outs, i: jnp.stack([o[i] for o in outs])
    by_token = lambda a: jnp.swapaxes(a, 2, 3)
    return (yp.reshape(B, T, D), ys.reshape(Bs, 1, D),
            by_token(stack(prompt_outs, 0)), by_token(stack(prompt_outs, 1)), stack(prompt_outs, 2), stack(prompt_outs, 3),
            stack(sample_outs, 0), stack(sample_outs, 1), stack(sample_outs, 2), stack(sample_outs, 3),
            stack(sample_outs, 4))
```

```python
import functools

import jax
import jax.numpy as jnp
from jax import lax
from jax.experimental import pallas as pl
from jax.experimental.pallas import tpu as pltpu

F32 = jnp.float32
BF16 = jnp.bfloat16
EPS = 1e-6
LANES = 128
SUBLANES = 8
A_GROUPS = 4
A_WIDTH = A_GROUPS * LANES
A_CHUNK = 128
HEADS = 6
HW = HEADS * LANES
CONV_W = 4
ZAB_WIDTH = HW + LANES
GDN_CHUNK = 128
SB_TQ = 256
SB_TK = 256
SB_STREAMS = 3
PAGES_PER_STEP = 8
LOG2E = 1.4426950408889634
VMEM_LIMIT_BYTES = 56 * 1024 * 1024


def _tile(n, candidates):
    for c in candidates:
        if n % c == 0:
            return c
    return n


def _cparams(*sem):
    return pltpu.CompilerParams(dimension_semantics=sem, vmem_limit_bytes=VMEM_LIMIT_BYTES)


def _dot(a, b):
    return jnp.dot(a, b, preferred_element_type=F32)


def _dot_nt(a, b):
    return lax.dot_general(a, b, (((1,), (1,)), ((), ())), preferred_element_type=F32)


def _sigmoid(x):
    return 1.0 / (1.0 + jnp.exp(-x))


def _silu(x):
    return x * _sigmoid(x)


def _softplus(x):
    return jnp.maximum(x, 0.0) + jnp.log(1.0 + jnp.exp(-jnp.abs(x)))


def _rms_rows(x):
    return x * lax.rsqrt(jnp.mean(x * x, axis=-1, keepdims=True) + EPS)


def _l2n_rows(x):
    return x * lax.rsqrt(jnp.sum(x * x, axis=-1, keepdims=True) + EPS)


def _last_row_step(axis):
    return pl.program_id(axis) == pl.num_programs(axis) - 1


def _rms_cast_kernel(x_ref, xs_ref, g_ref, o_ref, os_ref):
    o_ref[...] = (_rms_rows(x_ref[...]) * g_ref[...]).astype(o_ref.dtype)

    @pl.when(_last_row_step(0))
    def _():
        os_ref[...] = (_rms_rows(xs_ref[...]) * g_ref[...]).astype(os_ref.dtype)


def _rms_cast(x, xs, gains, l):
    R, D = x.shape
    Bs = xs.shape[0]
    tr = _tile(R, (512, 256, 128, SUBLANES))
    rows = pl.BlockSpec((tr, D), lambda i: (i, 0))
    srows = pl.BlockSpec((Bs, D), lambda i: (0, 0))
    return pl.pallas_call(
        _rms_cast_kernel, grid=(R // tr,),
        in_specs=[rows, srows, pl.BlockSpec((None, 1, D), lambda i: (l, 0, 0))],
        out_specs=[rows, srows],
        out_shape=[jax.ShapeDtypeStruct((R, D), BF16), jax.ShapeDtypeStruct((Bs, D), BF16)],
        compiler_params=_cparams("arbitrary"), name="rms_cast")(x, xs, gains)


def _store_group(o_ref, c, val, head_major):
    if head_major:
        o_ref[c] = val.astype(o_ref.dtype)
    else:
        o_ref[:, c * LANES:(c + 1) * LANES] = val.astype(o_ref.dtype)


def _epilogue(acc, gain_ref, o_refs, head_major, *, act=None, act_cols=None, group_rms=False):
    for c in range(acc.shape[1] // LANES):
        cs = slice(c * LANES, (c + 1) * LANES)
        blk = acc[:, cs]
        if act is not None and (act_cols is None or c * LANES < act_cols):
            blk = act(blk)
        if group_rms:
            blk = _rms_rows(blk) * gain_ref[:, cs]
        for o_ref in o_refs:
            _store_group(o_ref, c, blk, head_major)


def _proj_kernel(h_ref, hs_ref, w_ref, *rest, epilogue, head_major, has_gain):
    gain_ref = rest[0] if has_gain else None
    o_refs, os_ref = rest[1 if has_gain else 0:-1], rest[-1]
    w = w_ref[...]
    epilogue(_dot(h_ref[...], w), gain_ref, o_refs, head_major)

    @pl.when(_last_row_step(1))
    def _():
        epilogue(_dot(hs_ref[...], w), gain_ref, (os_ref,), False)


def _proj(h, hs, w, l, epilogue, gain=None, head_major=None, out_dtypes=(F32,)):
    R, K = h.shape
    Bs = hs.shape[0]
    N = w.shape[-1]
    tm = _tile(R, (512, 256, 128, SUBLANES))
    tn = N if head_major else _tile(N, (1024, 896, 768, 512, 384, 256, LANES))
    in_specs = [pl.BlockSpec((tm, K), lambda j, i: (i, 0)),
                pl.BlockSpec((Bs, K), lambda j, i: (0, 0)),
                pl.BlockSpec((None, K, tn), lambda j, i: (l, 0, j))]
    args = [h, hs, w]
    if gain is not None:
        in_specs.append(pl.BlockSpec((None, 1, tn), lambda j, i: (l, 0, j)))
        args.append(gain)
    if head_major:
        B, T = head_major
        nb = T // tm
        out_spec = pl.BlockSpec((None, N // LANES, tm, LANES), lambda j, i: (i // nb, 0, i % nb, 0))
        shape = (B, N // LANES, T, LANES)
    else:
        out_spec = pl.BlockSpec((tm, tn), lambda j, i: (i, j))
        shape = (R, N)
    return pl.pallas_call(
        functools.partial(_proj_kernel, epilogue=epilogue, head_major=bool(head_major), has_gain=gain is not None),
        grid=(N // tn, R // tm), in_specs=in_specs,
        out_specs=[out_spec] * len(out_dtypes) + [pl.BlockSpec((Bs, tn), lambda j, i: (0, j))],
        out_shape=[jax.ShapeDtypeStruct(shape, dt) for dt in out_dtypes] + [jax.ShapeDtypeStruct((Bs, N), F32)],
        compiler_params=_cparams("parallel", "arbitrary"), name="proj")(*args)


def _mixer_a_kernel(gu_ref, vn_ref, ws_ref, bs_ref, o_ref, *, chunks):
    t = lax.broadcasted_iota(jnp.int32, (A_CHUNK, A_CHUNK), 0)
    s = lax.broadcasted_iota(jnp.int32, (A_CHUNK, A_CHUNK), 1)
    for g in range(A_GROUPS):
        cs = slice(g * LANES, (g + 1) * LANES)
        w = jnp.where(t >= s, ws_ref[g], 0.0).astype(BF16)
        for c in range(chunks):
            rs = slice(c * A_CHUNK, (c + 1) * A_CHUNK)
            mixed = _dot(w, vn_ref[rs, cs].astype(BF16)) + bs_ref[:, cs]
            o_ref[rs, cs] = (gu_ref[rs, cs] * mixed).astype(o_ref.dtype)


def _mixer_a(gu, vn, a_ws, bs_rows, l):
    R = gu.shape[0]
    chunks = _tile(R // A_CHUNK, (4, 2, 1))
    tr = chunks * A_CHUNK
    return pl.pallas_call(
        functools.partial(_mixer_a_kernel, chunks=chunks), grid=(R // tr,),
        in_specs=[pl.BlockSpec((tr, A_WIDTH), lambda i: (i, 0)),
                  pl.BlockSpec((tr, A_WIDTH), lambda i: (i, 0)),
                  pl.BlockSpec((None, A_GROUPS, A_CHUNK, A_CHUNK), lambda i: (l, 0, 0, 0)),
                  pl.BlockSpec((None, A_CHUNK, A_WIDTH), lambda i: (l, 0, 0))],
        out_specs=pl.BlockSpec((tr, A_WIDTH), lambda i: (i, 0)),
        out_shape=jax.ShapeDtypeStruct((R, A_WIDTH), BF16),
        compiler_params=_cparams("parallel"), name="mixer_a")(gu, vn, a_ws, bs_rows)


def _split3_bf16(x):
    a = x.astype(BF16)
    r = x - a.astype(F32)
    b = r.astype(BF16)
    c = (r - b.astype(F32)).astype(BF16)
    return a, b, c


def _gdn_prep_kernel(x_ref, halo_ref, cw_ref, ab_ref, alog_ref, dtb_ref,
                     q_ref, k_ref, v_ref, gb_ref, bb_ref, grow_ref, xp_ref, *, tr):
    i = pl.program_id(1)
    xp_ref[0:SUBLANES, :] = jnp.where(i == 0, 0.0, halo_ref[...])
    xp_ref[SUBLANES:, :] = x_ref[...]
    y = cw_ref[CONV_W - 1:CONV_W, :] * x_ref[...]
    for s in range(1, CONV_W):
        y = y + cw_ref[CONV_W - 1 - s:CONV_W - s, :] * xp_ref[pl.ds(SUBLANES - s, tr), :]
    y = _silu(y)
    for h in range(HEADS):
        hs = slice(h * LANES, (h + 1) * LANES)
        q_ref[:, hs] = _l2n_rows(y[:, hs]) * (LANES ** -0.5)
        k_ref[:, hs] = _l2n_rows(y[:, HW + h * LANES:HW + (h + 1) * LANES])
    v_ref[...] = y[:, 2 * HW:]
    ab = ab_ref[...]
    g = -jnp.exp(alog_ref[...]) * _softplus(ab + dtb_ref[...])
    beta = _sigmoid(ab)
    r = lax.broadcasted_iota(jnp.int32, (tr, tr), 0)
    c = lax.broadcasted_iota(jnp.int32, (tr, tr), 1)
    tri = jnp.where((r >= c) & (r // GDN_CHUNK == c // GDN_CHUNK), 1.0, 0.0).astype(BF16)
    gcum = sum(_dot(tri, part) for part in _split3_bf16(g))
    for h in range(HEADS):
        hs = slice(h * LANES, (h + 1) * LANES)
        gb_ref[:, hs] = jnp.broadcast_to(gcum[:, h:h + 1], (tr, LANES))
        bb_ref[:, hs] = jnp.broadcast_to(beta[:, HEADS + h:HEADS + h + 1], (tr, LANES))
    gt = gcum.T
    for h in range(HEADS):
        grow_ref[h] = gt[h:h + 1, :]


def _gdn_prep(bqkv, zab, conv_w, alog_rows, dtb_rows, B, T, l):
    tr = _tile(T, (256, GDN_CHUNK))
    nb = T // tr
    row = lambda b, i: (b * nb + i, 0)
    ab_group = lambda b, i: (b * nb + i, HW // LANES)
    halo = lambda b, i: (jnp.maximum((b * T + i * tr) // SUBLANES - 1, 0), 0)
    par = lambda b, i: (l, 0, 0)
    R = B * T
    wide = jax.ShapeDtypeStruct((R, HW), F32)
    return pl.pallas_call(
        functools.partial(_gdn_prep_kernel, tr=tr), grid=(B, nb),
        in_specs=[pl.BlockSpec((tr, 3 * HW), row),
                  pl.BlockSpec((SUBLANES, 3 * HW), halo),
                  pl.BlockSpec((None, CONV_W, 3 * HW), par),
                  pl.BlockSpec((tr, LANES), ab_group),
                  pl.BlockSpec((None, 1, LANES), par),
                  pl.BlockSpec((None, 1, LANES), par)],
        out_specs=[pl.BlockSpec((tr, HW), row)] * 5
        + [pl.BlockSpec((None, HEADS, 1, tr), lambda b, i: (b, 0, 0, i))],
        out_shape=[wide] * 5 + [jax.ShapeDtypeStruct((B, HEADS, 1, T), F32)],
        scratch_shapes=[pltpu.VMEM((tr + SUBLANES, 3 * HW), F32)],
        compiler_params=_cparams("parallel", "parallel"), name="gdn_prep",
    )(bqkv, bqkv, conv_w, zab, alog_rows, dtb_rows)


INV_BASE = 16


def _unit_lower_inverses_minus_eye(ns, r, c):
    L = ns[0].shape[0]
    shift = INV_BASE.bit_length() - 1
    same_block = (r >> shift) == (c >> shift)
    xos = [jnp.where(same_block, n, 0.0) for n in ns]
    nbs = [x.astype(BF16) for x in xos]
    ps = [_dot(nb, nb) for nb in nbs]
    for _ in range(shift - 2):
        boths = [_dot(jnp.concatenate([xo, p], axis=0).astype(BF16), p.astype(BF16)) for xo, p in zip(xos, ps)]
        xos = [xo + p + both[:L] for xo, p, both in zip(xos, ps, boths)]
        ps = [both[L:] for both in boths]
    xps = [_dot(xo.astype(BF16), p.astype(BF16)) for xo, p in zip(xos, ps)]
    xos = [xo + p + xp for xo, p, xp in zip(xos, ps, xps)]
    while (1 << shift) < L:
        below_left = ((r >> shift) ^ (c >> shift)) == 1
        offs = [jnp.where(below_left, n, 0.0) for n in ns]
        xbs = [xo.astype(BF16) for xo in xos]
        ys = [off + _dot(xb, off.astype(BF16)) for off, xb in zip(offs, xbs)]
        yxs = [_dot(y.astype(BF16), xb) for y, xb in zip(ys, xbs)]
        xos = [xo + y + yx for xo, y, yx in zip(xos, ys, yxs)]
        shift += 1
    return xos


def _gdn_chunk_updates(qs, ks, vs, gcs, bts, grows, states):
    L = qs[0].shape[0]
    r = lax.broadcasted_iota(jnp.int32, (L, L), 0)
    c = lax.broadcasted_iota(jnp.int32, (L, L), 1)
    decays = [jnp.where(r >= c, jnp.exp(gc - grow), 0.0) for gc, grow in zip(gcs, grows)]
    kqs = [_dot_nt(jnp.concatenate([k, q], axis=0).astype(BF16), k.astype(BF16)) for k, q in zip(ks, qs)]
    ns = [jnp.where(r > c, -(bt * kq[:L] * decay), 0.0) for bt, kq, decay in zip(bts, kqs, decays)]
    xos = _unit_lower_inverses_minus_eye(ns, r, c)
    gams = [jnp.exp(gc) for gc in gcs]
    bvs = [bt * v for bt, v in zip(bts, vs)]
    bgks = [bt * gam * k for bt, gam, k in zip(bts, gams, ks)]
    corrs = [_dot(xo.astype(BF16), jnp.concatenate([bv, bgk], axis=1).astype(BF16))
             for xo, bv, bgk in zip(xos, bvs, bgks)]
    wqs = [_dot(jnp.concatenate([bgk + corr[:, LANES:], gam * q], axis=0).astype(BF16), state.astype(BF16))
           for bgk, corr, gam, q, state in zip(bgks, corrs, gams, qs, states)]
    us = [(bv + corr[:, :LANES] - wq[:L]).astype(BF16) for bv, corr, wq in zip(bvs, corrs, wqs)]
    glasts = [gc[L - 1:L, :] for gc in gcs]
    kds = [(k * jnp.exp(glast - gc)).astype(BF16) for k, glast, gc in zip(ks, glasts, gcs)]
    outs = [wq[L:] + _dot((kq[L:] * decay).astype(BF16), u) for wq, kq, decay, u in zip(wqs, kqs, decays, us)]
    new_states = [jnp.exp(glast) * state + lax.dot_general(kd, u, (((0,), (0,)), ((), ())),
                                                           preferred_element_type=F32)
                  for glast, state, kd, u in zip(glasts, states, kds, us)]
    return outs, new_states


def _gdn_kernel(q_ref, k_ref, v_ref, gb_ref, bb_ref, grow_ref, z_ref, go_ref, y_ref, s_ref, *, B):
    @pl.when(pl.program_id(0) == 0)
    def _():
        s_ref[...] = jnp.zeros_like(s_ref)

    units = [(b, h, slice(h * LANES, (h + 1) * LANES)) for b in range(B) for h in range(HEADS)]
    outs, new_states = _gdn_chunk_updates(
        [q_ref[b, :, hs] for b, h, hs in units], [k_ref[b, :, hs] for b, h, hs in units],
        [v_ref[b, :, hs] for b, h, hs in units], [gb_ref[b, :, hs] for b, h, hs in units],
        [bb_ref[b, :, hs] for b, h, hs in units], [grow_ref[b, h] for b, h, hs in units],
        [s_ref[b, h] for b, h, hs in units])
    for (b, h, hs), out, new_state in zip(units, outs, new_states):
        s_ref[b, h] = new_state
        y_ref[b, :, hs] = (_rms_rows(out) * go_ref[...] * z_ref[b, :, hs]).astype(y_ref.dtype)


def _gdn(q, k, v, gb, bb, grow, zab, go_rows, B, T, l):
    L = GDN_CHUNK
    blk = pl.BlockSpec((B, L, HW), lambda i: (0, i, 0))
    three = lambda a: a.reshape(B, T, HW)
    return pl.pallas_call(
        functools.partial(_gdn_kernel, B=B), grid=(T // L,),
        in_specs=[blk] * 5 + [pl.BlockSpec((B, HEADS, 1, L), lambda i: (0, 0, 0, i)), blk,
                              pl.BlockSpec((None, 1, LANES), lambda i: (l, 0, 0))],
        out_specs=[blk, pl.BlockSpec((B, HEADS, LANES, LANES), lambda i: (0, 0, 0, 0))],
        out_shape=[jax.ShapeDtypeStruct((B, T, HW), BF16),
                   jax.ShapeDtypeStruct((B, HEADS, LANES, LANES), F32)],
        compiler_params=_cparams("arbitrary"), name="gdn",
    )(three(q), three(k), three(v), three(gb), three(bb), grow, zab.reshape(B, T, ZAB_WIDTH), go_rows)


def _softplus2(z):
    return jnp.maximum(z, jnp.log2(1.0 + jnp.exp2(jnp.minimum(z, 126.0))))


def _sb_blocks(qs, k_ref, v_ref, biases, jj, carries, accs, *, q_start=None):
    tq, tk = SB_TQ, SB_TK
    ks = pl.ds(pl.multiple_of(jj * tk, tk), tk)
    r = lax.broadcasted_iota(jnp.int32, (tk, tk), 0)
    s = lax.broadcasted_iota(jnp.int32, (tk, tk), 1)
    later = jnp.where(r > s, 1.0, 0.0).astype(BF16)
    if q_start is not None:
        mask = (jj * tk + lax.broadcasted_iota(jnp.int32, (tq, tk), 1)
                < q_start + lax.broadcasted_iota(jnp.int32, (tq, tk), 0))
    zs = [_dot_nt(q, k_ref[n, ks, :].astype(BF16)) + bias for n, (q, bias) in enumerate(zip(qs, biases))]
    sps = [_softplus2(z) for z in zs]
    if q_start is not None:
        sps = [jnp.where(mask, sp, 0.0) for sp in sps]
    tails = [_dot(sp.astype(BF16), later) for sp in sps]
    ws = [jnp.exp2((z - sp) - tail - carry) for z, sp, tail, carry in zip(zs, sps, tails, carries)]
    if q_start is not None:
        ws = [jnp.where(mask, w, 0.0) for w in ws]
    accs = [acc + _dot(w.astype(BF16), v_ref[n, ks, :].astype(BF16)) for n, (acc, w) in enumerate(zip(accs, ws))]
    carries = [carry + (tail[:, :1] + sp[:, :1]) for carry, tail, sp in zip(carries, tails, sps)]
    return tuple(carries), tuple(accs)


def _sb_kernel(q_ref, k_ref, v_ref, bias_ref, o_ref):
    i = pl.program_id(2)
    streams = range(SB_STREAMS)
    per_q = SB_TQ // SB_TK
    qs = [(q_ref[:, n * LANES:(n + 1) * LANES] * (LANES ** -0.5 * LOG2E)).astype(BF16) for n in streams]
    biases = [bias_ref[n] for n in streams]
    state = (tuple(jnp.zeros((SB_TQ, 1), F32) for _ in streams),
             tuple(jnp.zeros((SB_TQ, LANES), F32) for _ in streams))
    for d in range(per_q):
        state = _sb_blocks(qs, k_ref, v_ref, biases, (i + 1) * per_q - 1 - d, *state, q_start=i * SB_TQ)

    def body(step, state):
        return _sb_blocks(qs, k_ref, v_ref, biases, i * per_q - 1 - step, *state)

    carries, accs = lax.fori_loop(0, i * per_q, body, state)
    for n in streams:
        o_ref[:, n * LANES:(n + 1) * LANES] = accs[n].astype(o_ref.dtype)


def _sb_attention(qc, kc, vc, bias2_keys, B, T, l):
    assert SB_TQ % SB_TK == 0 and T % SB_TQ == 0 and HEADS % SB_STREAMS == 0
    width = SB_STREAMS * LANES
    kv = pl.BlockSpec((None, SB_STREAMS, T, LANES), lambda b, g, i: (b, g, 0, 0))
    qo = pl.BlockSpec((None, SB_TQ, width), lambda b, g, i: (b, i, g))
    return pl.pallas_call(
        _sb_kernel, grid=(B, HEADS // SB_STREAMS, T // SB_TQ),
        in_specs=[qo, kv, kv, pl.BlockSpec((None, SB_STREAMS, 1, SB_TK), lambda b, g, i: (l, g, 0, 0))],
        out_specs=qo, out_shape=jax.ShapeDtypeStruct((B, T, HW), BF16),
        compiler_params=_cparams("parallel", "parallel", "parallel"), name="sb_attention",
    )(qc.reshape(B, T, HW), kc, vc, bias2_keys)


def _sb_decode_kernel(pt_ref, q_ref, bias_ref, *rest):
    del pt_ref
    k_refs = rest[:PAGES_PER_STEP]
    v_refs = rest[PAGES_PER_STEP:2 * PAGES_PER_STEP]
    o_ref, carry_ref = rest[2 * PAGES_PER_STEP:]
    j = pl.program_id(1)

    @pl.when(j == 0)
    def _():
        o_ref[...] = jnp.zeros_like(o_ref)
        carry_ref[...] = jnp.zeros_like(carry_ref)

    qb = (q_ref[...] * (LANES ** -0.5 * LOG2E)).astype(BF16)
    bias = bias_ref[...]
    P = k_refs[0].shape[1]
    head = lax.broadcasted_iota(jnp.int32, (SUBLANES, LANES), 0)
    r = lax.broadcasted_iota(jnp.int32, (P, P), 0)
    s = lax.broadcasted_iota(jnp.int32, (P, P), 1)
    later = jnp.where(r > s, 1.0, 0.0).astype(BF16)
    pages = range(PAGES_PER_STEP)
    zhs = [[_dot_nt(qb, k_refs[p][h].astype(BF16)) for h in range(HEADS)] for p in pages]
    zs = []
    for p in pages:
        z = zhs[p][0]
        for h in range(1, HEADS):
            z = jnp.where(head == h, zhs[p][h], z)
        zs.append(z + bias)
    sps = [_softplus2(z) for z in zs]
    tails = [_dot(sp.astype(BF16), later) for sp in sps]
    carries = [carry_ref[...]]
    for p in pages:
        carries.append(carries[-1] + (tails[p][:, :1] + sps[p][:, :1]))
    ws = [jnp.exp2((zs[p] - sps[p]) - tails[p] - carries[p]).astype(BF16) for p in pages]
    ohs = [[_dot(ws[p], v_refs[p][h].astype(BF16)) for h in range(HEADS)] for p in pages]
    acc = o_ref[...]
    for h in range(HEADS):
        acc = acc + jnp.where(head == h, sum(ohs[p][h] for p in pages), 0.0)
    o_ref[...] = acc
    carry_ref[...] = carries[-1]


def _sb_decode(qc, cache_k, cache_v, page_table, bias2_lanes, l):
    Bs, NP = page_table.shape
    P = cache_k.shape[3]
    assert NP % PAGES_PER_STEP == 0 and P == LANES
    pt = page_table.reshape(-1)
    q = jnp.pad(qc.reshape(Bs, HEADS, LANES), ((0, 0), (0, SUBLANES - HEADS), (0, 0)))

    def page_spec(p):
        def index(b, j, pt_ref):
            return (l, pt_ref[b * NP + (NP - 1 - (j * PAGES_PER_STEP + p))], 0, 0, 0)
        return pl.BlockSpec((None, None, HEADS, P, LANES), index)

    row = pl.BlockSpec((None, SUBLANES, LANES), lambda b, j, pt_ref: (b, 0, 0))
    pages = [page_spec(p) for p in range(PAGES_PER_STEP)]
    out = pl.pallas_call(
        _sb_decode_kernel,
        grid_spec=pltpu.PrefetchScalarGridSpec(
            num_scalar_prefetch=1, grid=(Bs, NP // PAGES_PER_STEP),
            in_specs=[row, pl.BlockSpec((None, SUBLANES, LANES), lambda b, j, pt_ref: (l, 0, 0))] + pages + pages,
            out_specs=row, scratch_shapes=[pltpu.VMEM((SUBLANES, LANES), F32)]),
        out_shape=jax.ShapeDtypeStruct((Bs, SUBLANES, LANES), F32),
        compiler_params=_cparams("parallel", "arbitrary"), name="sb_decode",
    )(pt, q, bias2_lanes, *([cache_k] * PAGES_PER_STEP), *([cache_v] * PAGES_PER_STEP))
    return out[:, :HEADS].reshape(Bs, HW)


def _columns(x):
    pad = jnp.zeros((LANES - x.shape[0], LANES), F32)
    return jnp.concatenate([x, pad], axis=0).T


def _sample_mix_kernel(gu_ref, vn_ref, ws_ref, bs_ref, x_ref, hist_ref, cw_ref, ab_ref, alog_ref, dtb_ref,
                       z_ref, s_ref, go_ref, ya_ref, yb_ref, sn_ref, o_scr):
    Bs = gu_ref.shape[0]
    ya_ref[...] = gu_ref[...] * (ws_ref[...] * vn_ref[...] + bs_ref[...])
    y = cw_ref[CONV_W - 1:CONV_W, :] * x_ref[...]
    for s in range(CONV_W - 1):
        y = y + cw_ref[s:s + 1, :] * hist_ref[s]
    y = _silu(y)
    ab = ab_ref[...]
    decay = jnp.exp(-jnp.exp(alog_ref[...]) * _softplus(ab + dtb_ref[...]))
    beta = _sigmoid(ab)
    for h in range(HEADS):
        hs = slice(h * LANES, (h + 1) * LANES)
        q = _l2n_rows(y[:, hs]) * (LANES ** -0.5)
        k = _l2n_rows(y[:, HW + h * LANES:HW + (h + 1) * LANES])
        v = y[:, 2 * HW + h * LANES:2 * HW + (h + 1) * LANES]
        qcols, kcols = _columns(q), _columns(k)
        for b in range(Bs):
            a = decay[b:b + 1, h:h + 1]
            bt = beta[b:b + 1, HEADS + h:HEADS + h + 1]
            state = s_ref[b, h]
            kcol = kcols[:, b:b + 1]
            ks = jnp.sum(kcol * state, axis=0, keepdims=True)
            u = bt * (v[b:b + 1, :] - a * ks)
            new_state = a * state + kcol * u
            sn_ref[b, h] = new_state
            o_scr[b:b + 1, hs] = jnp.sum(qcols[:, b:b + 1] * new_state, axis=0, keepdims=True)
    for h in range(HEADS):
        hs = slice(h * LANES, (h + 1) * LANES)
        yb_ref[:, hs] = _rms_rows(o_scr[:, hs]) * go_ref[...] * z_ref[:, hs]


def _sample_mix(gu, vn, ws_rows, bs_rows, bqkv, hist, conv_w, zab, alog_rows, dtb_rows, state, go_rows, l):
    Bs = gu.shape[0]
    full = lambda shape: pl.BlockSpec(shape, lambda i: (0,) * len(shape))
    par = lambda shape: pl.BlockSpec((None,) + shape, lambda i: (l,) + (0,) * len(shape))
    ab, zsilu = zab, zab
    return pl.pallas_call(
        _sample_mix_kernel, grid=(1,),
        in_specs=[full((Bs, A_WIDTH)), full((Bs, A_WIDTH)), par((1, A_WIDTH)), par((1, A_WIDTH)),
                  full((Bs, 3 * HW)), par((CONV_W - 1, Bs, 3 * HW)), par((CONV_W, 3 * HW)),
                  pl.BlockSpec((Bs, LANES), lambda i: (0, HW // LANES)), par((1, LANES)), par((1, LANES)),
                  full((Bs, HW)), par((Bs, HEADS, LANES, LANES)), par((1, LANES))],
        out_specs=[full((Bs, A_WIDTH)), full((Bs, HW)), full((Bs, HEADS, LANES, LANES))],
        out_shape=[jax.ShapeDtypeStruct((Bs, A_WIDTH), F32), jax.ShapeDtypeStruct((Bs, HW), F32),
                   jax.ShapeDtypeStruct((Bs, HEADS, LANES, LANES), F32)],
        scratch_shapes=[pltpu.VMEM((Bs, HW), F32)],
        compiler_params=_cparams("arbitrary"), name="sample_mix",
    )(gu, vn, ws_rows, bs_rows, bqkv, hist, conv_w, ab, alog_rows, dtb_rows, zsilu, state, go_rows)


def _merge_kernel(ya_ref, yb_ref, yc_ref, ga_ref, gb_ref, gc_ref, yas_ref, ybs_ref, ycs_ref, gas_ref, gbs_ref, gcs_ref,
                  wa_ref, wb_ref, wc_ref, o_ref, os_ref):
    def merged(ya, yb, yc, ga, gb, gc):
        m = (ga[...] * _dot(ya[...].astype(BF16), wa_ref[...]) + gb[...] * _dot(yb[...].astype(BF16), wb_ref[...])
             + gc[...] * _dot(yc[...].astype(BF16), wc_ref[...]))
        return m.astype(o_ref.dtype)

    o_ref[...] = merged(ya_ref, yb_ref, yc_ref, ga_ref, gb_ref, gc_ref)

    @pl.when(_last_row_step(1))
    def _():
        os_ref[...] = merged(yas_ref, ybs_ref, ycs_ref, gas_ref, gbs_ref, gcs_ref)


def _merge(ys, ys_s, gates, gates_s, wa, wb, wc, l):
    R = ys[0].shape[0]
    Bs = ys_s[0].shape[0]
    D = wa.shape[-1]
    tm = _tile(R, (512, 256, 128, SUBLANES))
    tn = _tile(D, (1024, 512, 256, LANES))
    nj = D // tn
    rows = lambda width: pl.BlockSpec((tm, width), lambda j, i: (i, 0))
    srows = lambda width: pl.BlockSpec((Bs, width), lambda j, i: (0, 0))
    gate = lambda k: pl.BlockSpec((tm, tn), lambda j, i: (i, k * nj + j))
    sgate = lambda k: pl.BlockSpec((Bs, tn), lambda j, i: (0, k * nj + j))
    wspec = lambda width: pl.BlockSpec((None, width, tn), lambda j, i: (l, 0, j))
    return pl.pallas_call(
        _merge_kernel, grid=(nj, R // tm),
        in_specs=[rows(A_WIDTH), rows(HW), rows(HW), gate(0), gate(1), gate(2),
                  srows(A_WIDTH), srows(HW), srows(HW), sgate(0), sgate(1), sgate(2),
                  wspec(A_WIDTH), wspec(HW), wspec(HW)],
        out_specs=[pl.BlockSpec((tm, tn), lambda j, i: (i, j)), pl.BlockSpec((Bs, tn), lambda j, i: (0, j))],
        out_shape=[jax.ShapeDtypeStruct((R, D), BF16), jax.ShapeDtypeStruct((Bs, D), BF16)],
        compiler_params=_cparams("parallel", "arbitrary"), name="merge",
    )(*ys, gates, gates, gates, *ys_s, gates_s, gates_s, gates_s, wa, wb, wc)


def _outproj_kernel(m_ref, x_ref, ms_ref, xs_ref, w_ref, g_ref, x1_ref, h2_ref, x1s_ref, h2s_ref):
    def project(m, x, x1_out, h2_out):
        x1 = x + _dot(m, w_ref[...])
        x1_out[...] = x1
        h2_out[...] = (_rms_rows(x1) * g_ref[...]).astype(h2_out.dtype)

    project(m_ref[...], x_ref[...], x1_ref, h2_ref)

    @pl.when(_last_row_step(0))
    def _():
        project(ms_ref[...], xs_ref[...], x1s_ref, h2s_ref)


def _outproj(m, x, m_s, x_s, w_out, gains, l):
    R, D = x.shape
    Bs = x_s.shape[0]
    tm = _tile(R, (256, 128, SUBLANES))
    rows = pl.BlockSpec((tm, D), lambda i: (i, 0))
    srows = pl.BlockSpec((Bs, D), lambda i: (0, 0))
    return pl.pallas_call(
        _outproj_kernel, grid=(R // tm,),
        in_specs=[rows, rows, srows, srows, pl.BlockSpec((None, D, D), lambda i: (l, 0, 0)),
                  pl.BlockSpec((None, 1, D), lambda i: (l, 0, 0))],
        out_specs=[rows, rows, srows, srows],
        out_shape=[jax.ShapeDtypeStruct((R, D), F32), jax.ShapeDtypeStruct((R, D), BF16),
                   jax.ShapeDtypeStruct((Bs, D), F32), jax.ShapeDtypeStruct((Bs, D), BF16)],
        compiler_params=_cparams("arbitrary"), name="outproj")(m, x, m_s, x_s, w_out, gains)


def _ffn_up_kernel(h_ref, hs_ref, wg_ref, wu_ref, o_ref, os_ref, wgb_ref, wub_ref):
    @pl.when(pl.program_id(1) == 0)
    def _():
        wgb_ref[...] = wg_ref[...].astype(BF16)
        wub_ref[...] = wu_ref[...].astype(BF16)

    def swiglu(h):
        return _silu(_dot(h, wgb_ref[...])) * _dot(h, wub_ref[...])

    o_ref[...] = swiglu(h_ref[...]).astype(o_ref.dtype)

    @pl.when(_last_row_step(1))
    def _():
        os_ref[...] = swiglu(hs_ref[...]).astype(os_ref.dtype)


def _ffn_up(h2, h2_s, w_gate, w_up, l):
    R, D = h2.shape
    Bs = h2_s.shape[0]
    F = w_gate.shape[-1]
    tm = _tile(R, (512, 256, 128, SUBLANES))
    tn = _tile(F, (512, 256, LANES))
    wspec = pl.BlockSpec((None, D, tn), lambda j, i: (l, 0, j))
    return pl.pallas_call(
        _ffn_up_kernel, grid=(F // tn, R // tm),
        in_specs=[pl.BlockSpec((tm, D), lambda j, i: (i, 0)), pl.BlockSpec((Bs, D), lambda j, i: (0, 0)),
                  wspec, wspec],
        out_specs=[pl.BlockSpec((tm, tn), lambda j, i: (i, j)), pl.BlockSpec((Bs, tn), lambda j, i: (0, j))],
        out_shape=[jax.ShapeDtypeStruct((R, F), BF16), jax.ShapeDtypeStruct((Bs, F), BF16)],
        scratch_shapes=[pltpu.VMEM((D, tn), BF16), pltpu.VMEM((D, tn), BF16)],
        compiler_params=_cparams("arbitrary", "arbitrary"), name="ffn_up")(h2, h2_s, w_gate, w_up)


def _ffn_down_kernel(a_ref, x_ref, as_ref, xs_ref, w_ref, o_ref, os_ref, wb_ref):
    @pl.when(pl.program_id(1) == 0)
    def _():
        wb_ref[...] = w_ref[...].astype(BF16)

    o_ref[...] = x_ref[...] + _dot(a_ref[...], wb_ref[...])

    @pl.when(_last_row_step(1))
    def _():
        os_ref[...] = xs_ref[...] + _dot(as_ref[...], wb_ref[...])


def _ffn_down(act, x1, act_s, x1_s, w_down, l):
    R, F = act.shape
    Bs = act_s.shape[0]
    D = x1.shape[-1]
    tm = _tile(R, (512, 256, 128, SUBLANES))
    tn = _tile(D, (512, 256, LANES))
    tile = pl.BlockSpec((tm, tn), lambda j, i: (i, j))
    stile = pl.BlockSpec((Bs, tn), lambda j, i: (0, j))
    return pl.pallas_call(
        _ffn_down_kernel, grid=(D // tn, R // tm),
        in_specs=[pl.BlockSpec((tm, F), lambda j, i: (i, 0)), tile,
                  pl.BlockSpec((Bs, F), lambda j, i: (0, 0)), stile,
                  pl.BlockSpec((None, F, tn), lambda j, i: (l, 0, j))],
        out_specs=[tile, stile],
        out_shape=[jax.ShapeDtypeStruct((R, D), F32), jax.ShapeDtypeStruct((Bs, D), F32)],
        scratch_shapes=[pltpu.VMEM((F, tn), BF16)],
        compiler_params=_cparams("arbitrary", "arbitrary"), name="ffn_down")(act, x1, act_s, x1_s, w_down)


def _prepare_params(d_model, norm_mix, w_in, a_vnorm, a_ws, a_bs, dn_conv, dn_a_log, dn_dt_bias, dn_onorm,
                    sb_qnorm, sb_knorm, sb_bias, w_branch, w_out, norm_ffn, w_gate, w_up, w_down):
    depth = w_in.shape[0]
    sizes = (A_WIDTH, A_WIDTH, 3 * HW, HW + 2 * HEADS, HW, HW, HW, 3 * d_model)
    names = ("a_u", "a_v", "b_qkv", "b_zab", "c_q", "c_k", "c_v", "gates")
    p, lo = {}, 0
    for name, size in zip(names, sizes, strict=True):
        p["w_" + name] = w_in[:, :, lo:lo + size].astype(BF16)
        lo += size
    p["w_b_zab"] = jnp.pad(p["w_b_zab"], ((0, 0), (0, 0), (0, ZAB_WIDTH - HW - 2 * HEADS)))
    pad_heads = lambda a: jnp.pad(a, ((0, 0), (0, LANES - HEADS)))[:, None, :]
    tile_heads = lambda a: jnp.tile(a, (1, HEADS))[:, None, :]
    p.update(
        norm_mix=norm_mix[:, None, :], norm_ffn=norm_ffn[:, None, :],
        a_vnorm=a_vnorm.reshape(depth, 1, A_WIDTH), a_ws=a_ws,
        a_bs_rows=jnp.repeat(jnp.swapaxes(a_bs, 1, 2), LANES, axis=2),
        a_ws0=jnp.repeat(a_ws[:, :, 0, 0], LANES, axis=1)[:, None, :],
        a_bs0=jnp.repeat(a_bs[:, :, 0], LANES, axis=1)[:, None, :],
        dn_conv=dn_conv, alog=pad_heads(dn_a_log), dtb=pad_heads(dn_dt_bias),
        dn_onorm=dn_onorm[:, None, :], sb_qnorm=tile_heads(sb_qnorm), sb_knorm=tile_heads(sb_knorm),
        sb_bias2_keys=jnp.broadcast_to((sb_bias * LOG2E)[:, :, None, None], sb_bias.shape + (1, SB_TK)),
        sb_bias2_lanes=jnp.broadcast_to(jnp.pad(sb_bias * LOG2E, ((0, 0), (0, SUBLANES - HEADS)))[:, :, None],
                                        (depth, SUBLANES, LANES)),
        wb_a=w_branch[:, :A_WIDTH].astype(BF16), wb_b=w_branch[:, A_WIDTH:A_WIDTH + HW].astype(BF16),
        wb_c=w_branch[:, A_WIDTH + HW:].astype(BF16), w_out=w_out.astype(BF16),
        w_gate=w_gate, w_up=w_up, w_down=w_down)
    return p


def _project(x, xs, p, l, B, T):
    h, hs = _rms_cast(x, xs, p["norm_mix"], l)
    epi = lambda **kw: functools.partial(_epilogue, **kw)
    calls = dict(
        gu=_proj(h, hs, p["w_a_u"], l, epi(act=jax.nn.gelu)),
        vn=_proj(h, hs, p["w_a_v"], l, epi(act=jax.nn.gelu, group_rms=True), p["a_vnorm"], out_dtypes=(BF16,)),
        bqkv=_proj(h, hs, p["w_b_qkv"], l, epi()),
        zab=_proj(h, hs, p["w_b_zab"], l, epi(act=_silu, act_cols=HW)),
        qc=_proj(h, hs, p["w_c_q"], l, epi(group_rms=True), p["sb_qnorm"]),
        kc=_proj(h, hs, p["w_c_k"], l, epi(group_rms=True), p["sb_knorm"], head_major=(B, T), out_dtypes=(F32, BF16)),
        vc=_proj(h, hs, p["w_c_v"], l, epi(), head_major=(B, T), out_dtypes=(F32, BF16)),
        gates=_proj(h, hs, p["w_gates"], l, epi(act=_sigmoid), out_dtypes=(BF16,)))
    prompt = {k: v[0] for k, v in calls.items()}
    prompt.update(kc_bf16=calls["kc"][1], vc_bf16=calls["vc"][1])
    return prompt, {k: v[-1] for k, v in calls.items()}


def _layer(x, xs, p, l, B, T, cache_k, cache_v, page_table, state_delta, state_conv):
    Bs = xs.shape[0]
    s, ss = _project(x, xs, p, l, B, T)
    ya = _mixer_a(s["gu"], s["vn"], p["a_ws"], p["a_bs_rows"], l)
    q, k, v, gb, bb, grow = _gdn_prep(s["bqkv"], s["zab"], p["dn_conv"], p["alog"], p["dtb"], B, T, l)
    yb, state = _gdn(q, k, v, gb, bb, grow, s["zab"], p["dn_onorm"], B, T, l)
    yc = _sb_attention(s["qc"], s["kc_bf16"], s["vc_bf16"], p["sb_bias2_keys"], B, T, l)
    hist = jnp.swapaxes(state_conv, 1, 2)
    ya_s, yb_s, state_s = _sample_mix(ss["gu"], ss["vn"], p["a_ws0"], p["a_bs0"], ss["bqkv"], hist, p["dn_conv"],
                                      ss["zab"], p["alog"], p["dtb"], state_delta, p["dn_onorm"], l)
    yc_s = _sb_decode(ss["qc"], cache_k, cache_v, page_table, p["sb_bias2_lanes"], l)
    m, m_s = _merge((ya, yb.reshape(B * T, HW), yc.reshape(B * T, HW)), (ya_s, yb_s, yc_s), s["gates"], ss["gates"],
                    p["wb_a"], p["wb_b"], p["wb_c"], l)
    x1, h2, x1_s, h2_s = _outproj(m, x, m_s, xs, p["w_out"], p["norm_ffn"], l)
    act, act_s = _ffn_up(h2, h2_s, p["w_gate"], p["w_up"], l)
    y, y_s = _ffn_down(act, x1, act_s, x1_s, p["w_down"], l)
    conv_tail = s["bqkv"].reshape(B, T, 3 * HW)[:, T - (CONV_W - 1):]
    conv_s = jnp.concatenate([state_conv[l][:, 1:], ss["bqkv"][:, None, :]], axis=1)
    prompt_outs = (s["kc"], s["vc"], state, conv_tail)
    sample_outs = (ss["kc"].reshape(Bs, 1, HEADS, LANES), ss["vc"].reshape(Bs, 1, HEADS, LANES), state_s, conv_s,
                   ss["vn"][:, None, :])
    return y, y_s, prompt_outs, sample_outs


def kernel(x_prompt, x_sample, cache_k, cache_v, state_delta, state_conv, page_table, norm_mix, w_in, a_vnorm, a_ws, a_bs, dn_conv, dn_a_log, dn_dt_bias, dn_onorm, sb_qnorm, sb_knorm, sb_bias, w_branch, w_out, norm_ffn, w_gate, w_up, w_down):
    B, T, D = x_prompt.shape
    Bs = x_sample.shape[0]
    depth = w_in.shape[0]
    assert x_sample.shape[1] == 1 and T % SB_TQ == 0 and T % GDN_CHUNK == 0
    p = _prepare_params(D, norm_mix, w_in, a_vnorm, a_ws, a_bs, dn_conv, dn_a_log, dn_dt_bias, dn_onorm,
                        sb_qnorm, sb_knorm, sb_bias, w_branch, w_out, norm_ffn, w_gate, w_up, w_down)
    cache_k = jnp.swapaxes(cache_k, 2, 3)
    cache_v = jnp.swapaxes(cache_v, 2, 3)
    yp = x_prompt.reshape(B * T, D)
    ys = x_sample.reshape(Bs, D)
    prompt_outs, sample_outs = [], []
    for l in range(depth):
        yp, ys, outs, outs_s = _layer(yp, ys, p, l, B, T, cache_k, cache_v, page_table, state_delta, state_conv)
        prompt_outs.append(outs)
        sample_outs.append(outs_s)
    stack = lambda outs, i: jnp.stack([o[i] for o in outs])
    by_token = lambda a: jnp.swapaxes(a, 2, 3)
    return (yp.reshape(B, T, D), ys.reshape(Bs, 1, D),
            by_token(stack(prompt_outs, 0)), by_token(stack(prompt_outs, 1)), stack(prompt_outs, 2), stack(prompt_outs, 3),
            stack(sample_outs, 0), stack(sample_outs, 1), stack(sample_outs, 2), stack(sample_outs, 3),
            stack(sample_outs, 4))
```

```python
import functools

import jax
import jax.numpy as jnp
from jax import lax
from jax.experimental import pallas as pl
from jax.experimental.pallas import tpu as pltpu

F32 = jnp.float32
BF16 = jnp.bfloat16
EPS = 1e-6
LANES = 128
SUBLANES = 8
A_GROUPS = 4
A_WIDTH = A_GROUPS * LANES
A_CHUNK = 128
HEADS = 6
HW = HEADS * LANES
CONV_W = 4
ZAB_WIDTH = HW + LANES
GDN_CHUNK = 128
SB_TQ = 256
SB_TK = 256
SB_STREAMS = 6
PAGES_PER_STEP = 16
ROW_TILES = (1024, 512, 256, 128, SUBLANES)
LOG2E = 1.4426950408889634
VMEM_LIMIT_BYTES = 56 * 1024 * 1024


def _tile(n, candidates):
    for c in candidates:
        if n % c == 0:
            return c
    return n


def _cparams(*sem):
    return pltpu.CompilerParams(dimension_semantics=sem, vmem_limit_bytes=VMEM_LIMIT_BYTES)


def _dot(a, b):
    return jnp.dot(a, b, preferred_element_type=F32)


def _dot_nt(a, b):
    return lax.dot_general(a, b, (((1,), (1,)), ((), ())), preferred_element_type=F32)


def _sigmoid(x):
    return 1.0 / (1.0 + jnp.exp(-x))


def _silu(x):
    return x * _sigmoid(x)


def _softplus(x):
    return jnp.maximum(x, 0.0) + jnp.log(1.0 + jnp.exp(-jnp.abs(x)))


def _rms_rows(x):
    return x * lax.rsqrt(jnp.mean(x * x, axis=-1, keepdims=True) + EPS)


def _l2n_rows(x):
    return x * lax.rsqrt(jnp.sum(x * x, axis=-1, keepdims=True) + EPS)


def _last_row_step(axis):
    return pl.program_id(axis) == pl.num_programs(axis) - 1


def _rms_cast_kernel(x_ref, xs_ref, g_ref, o_ref, os_ref):
    o_ref[...] = (_rms_rows(x_ref[...]) * g_ref[...]).astype(o_ref.dtype)

    @pl.when(_last_row_step(0))
    def _():
        os_ref[...] = (_rms_rows(xs_ref[...]) * g_ref[...]).astype(os_ref.dtype)


def _rms_cast(x, xs, gains, l):
    R, D = x.shape
    Bs = xs.shape[0]
    tr = _tile(R, ROW_TILES)
    rows = pl.BlockSpec((tr, D), lambda i: (i, 0))
    srows = pl.BlockSpec((Bs, D), lambda i: (0, 0))
    return pl.pallas_call(
        _rms_cast_kernel, grid=(R // tr,),
        in_specs=[rows, srows, pl.BlockSpec((None, 1, D), lambda i: (l, 0, 0))],
        out_specs=[rows, srows],
        out_shape=[jax.ShapeDtypeStruct((R, D), BF16), jax.ShapeDtypeStruct((Bs, D), BF16)],
        compiler_params=_cparams("arbitrary"), name="rms_cast")(x, xs, gains)


def _store_group(o_ref, c, val, head_major):
    if head_major:
        o_ref[c] = val.astype(o_ref.dtype)
    else:
        o_ref[:, c * LANES:(c + 1) * LANES] = val.astype(o_ref.dtype)


def _epilogue(acc, gain_ref, o_refs, head_major, *, act=None, act_cols=None, group_rms=False):
    for c in range(acc.shape[1] // LANES):
        cs = slice(c * LANES, (c + 1) * LANES)
        blk = acc[:, cs]
        if act is not None and (act_cols is None or c * LANES < act_cols):
            blk = act(blk)
        if group_rms:
            blk = _rms_rows(blk) * gain_ref[:, cs]
        for o_ref in o_refs:
            _store_group(o_ref, c, blk, head_major)


def _proj_kernel(h_ref, hs_ref, w_ref, *rest, epilogue, head_major, has_gain):
    gain_ref = rest[0] if has_gain else None
    o_refs, os_ref = rest[1 if has_gain else 0:-1], rest[-1]
    w = w_ref[...]
    epilogue(_dot(h_ref[...], w), gain_ref, o_refs, head_major)

    @pl.when(_last_row_step(1))
    def _():
        epilogue(_dot(hs_ref[...], w), gain_ref, (os_ref,), False)


def _proj(h, hs, w, l, epilogue, gain=None, head_major=None, out_dtypes=(F32,)):
    R, K = h.shape
    Bs = hs.shape[0]
    N = w.shape[-1]
    tm = _tile(head_major[1] if head_major else R, ROW_TILES)
    tn = N if head_major else _tile(N, (1024, 896, 768, 512, 384, 256, LANES))
    in_specs = [pl.BlockSpec((tm, K), lambda j, i: (i, 0)),
                pl.BlockSpec((Bs, K), lambda j, i: (0, 0)),
                pl.BlockSpec((None, K, tn), lambda j, i: (l, 0, j))]
    args = [h, hs, w]
    if gain is not None:
        in_specs.append(pl.BlockSpec((None, 1, tn), lambda j, i: (l, 0, j)))
        args.append(gain)
    if head_major:
        B, T = head_major
        nb = T // tm
        out_spec = pl.BlockSpec((None, N // LANES, tm, LANES), lambda j, i: (i // nb, 0, i % nb, 0))
        shape = (B, N // LANES, T, LANES)
    else:
        out_spec = pl.BlockSpec((tm, tn), lambda j, i: (i, j))
        shape = (R, N)
    return pl.pallas_call(
        functools.partial(_proj_kernel, epilogue=epilogue, head_major=bool(head_major), has_gain=gain is not None),
        grid=(N // tn, R // tm), in_specs=in_specs,
        out_specs=[out_spec] * len(out_dtypes) + [pl.BlockSpec((Bs, tn), lambda j, i: (0, j))],
        out_shape=[jax.ShapeDtypeStruct(shape, dt) for dt in out_dtypes] + [jax.ShapeDtypeStruct((Bs, N), F32)],
        compiler_params=_cparams("parallel", "arbitrary"), name="proj")(*args)


def _mixer_a_kernel(gu_ref, vn_ref, ws_ref, bs_ref, o_ref, *, chunks):
    t = lax.broadcasted_iota(jnp.int32, (A_CHUNK, A_CHUNK), 0)
    s = lax.broadcasted_iota(jnp.int32, (A_CHUNK, A_CHUNK), 1)
    for g in range(A_GROUPS):
        cs = slice(g * LANES, (g + 1) * LANES)
        w = jnp.where(t >= s, ws_ref[g], 0.0).astype(BF16)
        for c in range(chunks):
            rs = slice(c * A_CHUNK, (c + 1) * A_CHUNK)
            mixed = _dot(w, vn_ref[rs, cs].astype(BF16)) + bs_ref[:, cs]
            o_ref[rs, cs] = (gu_ref[rs, cs] * mixed).astype(o_ref.dtype)


def _mixer_a(gu, vn, a_ws, bs_rows, l):
    R = gu.shape[0]
    chunks = _tile(R // A_CHUNK, (4, 2, 1))
    tr = chunks * A_CHUNK
    return pl.pallas_call(
        functools.partial(_mixer_a_kernel, chunks=chunks), grid=(R // tr,),
        in_specs=[pl.BlockSpec((tr, A_WIDTH), lambda i: (i, 0)),
                  pl.BlockSpec((tr, A_WIDTH), lambda i: (i, 0)),
                  pl.BlockSpec((None, A_GROUPS, A_CHUNK, A_CHUNK), lambda i: (l, 0, 0, 0)),
                  pl.BlockSpec((None, A_CHUNK, A_WIDTH), lambda i: (l, 0, 0))],
        out_specs=pl.BlockSpec((tr, A_WIDTH), lambda i: (i, 0)),
        out_shape=jax.ShapeDtypeStruct((R, A_WIDTH), BF16),
        compiler_params=_cparams("parallel"), name="mixer_a")(gu, vn, a_ws, bs_rows)


def _split3_bf16(x):
    a = x.astype(BF16)
    r = x - a.astype(F32)
    b = r.astype(BF16)
    c = (r - b.astype(F32)).astype(BF16)
    return a, b, c


def _gdn_prep_kernel(x_ref, halo_ref, cw_ref, ab_ref, alog_ref, dtb_ref,
                     q_ref, k_ref, v_ref, gbeta_ref, grow_ref, xp_ref, *, tr):
    i = pl.program_id(1)
    xp_ref[0:SUBLANES, :] = jnp.where(i == 0, 0.0, halo_ref[...])
    xp_ref[SUBLANES:, :] = x_ref[...]
    y = cw_ref[CONV_W - 1:CONV_W, :] * x_ref[...]
    for s in range(1, CONV_W):
        y = y + cw_ref[CONV_W - 1 - s:CONV_W - s, :] * xp_ref[pl.ds(SUBLANES - s, tr), :]
    y = _silu(y)
    for h in range(HEADS):
        hs = slice(h * LANES, (h + 1) * LANES)
        q_ref[:, hs] = _l2n_rows(y[:, hs]) * (LANES ** -0.5)
        k_ref[:, hs] = _l2n_rows(y[:, HW + h * LANES:HW + (h + 1) * LANES])
    v_ref[...] = y[:, 2 * HW:]
    ab = ab_ref[...]
    g = -jnp.exp(alog_ref[...]) * _softplus(ab + dtb_ref[...])
    beta = _sigmoid(ab)
    r = lax.broadcasted_iota(jnp.int32, (tr, tr), 0)
    c = lax.broadcasted_iota(jnp.int32, (tr, tr), 1)
    tri = jnp.where((r >= c) & (r // GDN_CHUNK == c // GDN_CHUNK), 1.0, 0.0).astype(BF16)
    gcum = sum(_dot(tri, part) for part in _split3_bf16(g))
    lane = lax.broadcasted_iota(jnp.int32, (tr, LANES), 1)
    gbeta_ref[...] = jnp.where(lane < HEADS, gcum, beta)
    gt = gcum.T
    for h in range(HEADS):
        grow_ref[h] = gt[h:h + 1, :]


def _gdn_prep(bqkv, zab, conv_w, alog_rows, dtb_rows, B, T, l):
    tr = _tile(T, (256, GDN_CHUNK))
    nb = T // tr
    row = lambda b, i: (b * nb + i, 0)
    ab_group = lambda b, i: (b * nb + i, HW // LANES)
    halo = lambda b, i: (jnp.maximum((b * T + i * tr) // SUBLANES - 1, 0), 0)
    par = lambda b, i: (l, 0, 0)
    R = B * T
    wide = jax.ShapeDtypeStruct((R, HW), F32)
    return pl.pallas_call(
        functools.partial(_gdn_prep_kernel, tr=tr), grid=(B, nb),
        in_specs=[pl.BlockSpec((tr, 3 * HW), row),
                  pl.BlockSpec((SUBLANES, 3 * HW), halo),
                  pl.BlockSpec((None, CONV_W, 3 * HW), par),
                  pl.BlockSpec((tr, LANES), ab_group),
                  pl.BlockSpec((None, 1, LANES), par),
                  pl.BlockSpec((None, 1, LANES), par)],
        out_specs=[pl.BlockSpec((tr, HW), row)] * 3 + [pl.BlockSpec((tr, LANES), row)]
        + [pl.BlockSpec((None, HEADS, 1, tr), lambda b, i: (b, 0, 0, i))],
        out_shape=[wide] * 3 + [jax.ShapeDtypeStruct((R, LANES), F32), jax.ShapeDtypeStruct((B, HEADS, 1, T), F32)],
        scratch_shapes=[pltpu.VMEM((tr + SUBLANES, 3 * HW), F32)],
        compiler_params=_cparams("parallel", "parallel"), name="gdn_prep",
    )(bqkv, bqkv, conv_w, zab, alog_rows, dtb_rows)


INV_BASE = 16


def _unit_lower_inverses_minus_eye(ns, r, c):
    L = ns[0].shape[0]
    shift = INV_BASE.bit_length() - 1
    same_block = (r >> shift) == (c >> shift)
    xos = [jnp.where(same_block, n, 0.0) for n in ns]
    nbs = [x.astype(BF16) for x in xos]
    ps = [_dot(nb, nb) for nb in nbs]
    for _ in range(shift - 2):
        boths = [_dot(jnp.concatenate([xo, p], axis=0).astype(BF16), p.astype(BF16)) for xo, p in zip(xos, ps)]
        xos = [xo + p + both[:L] for xo, p, both in zip(xos, ps, boths)]
        ps = [both[L:] for both in boths]
    xps = [_dot(xo.astype(BF16), p.astype(BF16)) for xo, p in zip(xos, ps)]
    xos = [xo + p + xp for xo, p, xp in zip(xos, ps, xps)]
    while (1 << shift) < L:
        below_left = ((r >> shift) ^ (c >> shift)) == 1
        offs = [jnp.where(below_left, n, 0.0) for n in ns]
        xbs = [xo.astype(BF16) for xo in xos]
        ys = [off + _dot(xb, off.astype(BF16)) for off, xb in zip(offs, xbs)]
        yxs = [_dot(y.astype(BF16), xb) for y, xb in zip(ys, xbs)]
        xos = [xo + y + yx for xo, y, yx in zip(xos, ys, yxs)]
        shift += 1
    return xos


def _gdn_chunk_updates(qs, ks, vs, gcs, bts, grows, states):
    L = qs[0].shape[0]
    r = lax.broadcasted_iota(jnp.int32, (L, L), 0)
    c = lax.broadcasted_iota(jnp.int32, (L, L), 1)
    decays = [jnp.where(r >= c, jnp.exp(gc - grow), 0.0) for gc, grow in zip(gcs, grows)]
    kqs = [_dot_nt(jnp.concatenate([k, q], axis=0).astype(BF16), k.astype(BF16)) for k, q in zip(ks, qs)]
    ns = [jnp.where(r > c, -(bt * kq[:L] * decay), 0.0) for bt, kq, decay in zip(bts, kqs, decays)]
    xos = _unit_lower_inverses_minus_eye(ns, r, c)
    gams = [jnp.exp(gc) for gc in gcs]
    bvs = [bt * v for bt, v in zip(bts, vs)]
    bgks = [bt * gam * k for bt, gam, k in zip(bts, gams, ks)]
    corrs = [_dot(xo.astype(BF16), jnp.concatenate([bv, bgk], axis=1).astype(BF16))
             for xo, bv, bgk in zip(xos, bvs, bgks)]
    wqs = [_dot(jnp.concatenate([bgk + corr[:, LANES:], gam * q], axis=0).astype(BF16), state.astype(BF16))
           for bgk, corr, gam, q, state in zip(bgks, corrs, gams, qs, states)]
    us = [(bv + corr[:, :LANES] - wq[:L]).astype(BF16) for bv, corr, wq in zip(bvs, corrs, wqs)]
    glasts = [gc[L - 1:L, :] for gc in gcs]
    kds = [(k * jnp.exp(glast - gc)).astype(BF16) for k, glast, gc in zip(ks, glasts, gcs)]
    outs = [wq[L:] + _dot((kq[L:] * decay).astype(BF16), u) for wq, kq, decay, u in zip(wqs, kqs, decays, us)]
    new_states = [jnp.exp(glast) * state + lax.dot_general(kd, u, (((0,), (0,)), ((), ())),
                                                           preferred_element_type=F32)
                  for glast, state, kd, u in zip(glasts, states, kds, us)]
    return outs, new_states


def _gdn_kernel(q_ref, k_ref, v_ref, gbeta_ref, grow_ref, z_ref, go_ref, y_ref, s_ref, *, B):
    @pl.when(pl.program_id(0) == 0)
    def _():
        s_ref[...] = jnp.zeros_like(s_ref)

    units = [(b, h, slice(h * LANES, (h + 1) * LANES)) for b in range(B) for h in range(HEADS)]
    L = q_ref.shape[1]
    across = lambda b, lane: jnp.broadcast_to(gbeta_ref[b, :, lane:lane + 1], (L, LANES))
    outs, new_states = _gdn_chunk_updates(
        [q_ref[b, :, hs] for b, h, hs in units], [k_ref[b, :, hs] for b, h, hs in units],
        [v_ref[b, :, hs] for b, h, hs in units], [across(b, h) for b, h, hs in units],
        [across(b, HEADS + h) for b, h, hs in units], [grow_ref[b, h] for b, h, hs in units],
        [s_ref[b, h] for b, h, hs in units])
    for (b, h, hs), out, new_state in zip(units, outs, new_states):
        s_ref[b, h] = new_state
        y_ref[b, :, hs] = (_rms_rows(out) * go_ref[...] * z_ref[b, :, hs]).astype(y_ref.dtype)


def _gdn(q, k, v, gbeta, grow, zab, go_rows, B, T, l):
    L = GDN_CHUNK
    blk = pl.BlockSpec((B, L, HW), lambda i: (0, i, 0))
    three = lambda a: a.reshape(B, T, HW)
    return pl.pallas_call(
        functools.partial(_gdn_kernel, B=B), grid=(T // L,),
        in_specs=[blk] * 3 + [pl.BlockSpec((B, L, LANES), lambda i: (0, i, 0)),
                              pl.BlockSpec((B, HEADS, 1, L), lambda i: (0, 0, 0, i)), blk,
                              pl.BlockSpec((None, 1, LANES), lambda i: (l, 0, 0))],
        out_specs=[blk, pl.BlockSpec((B, HEADS, LANES, LANES), lambda i: (0, 0, 0, 0))],
        out_shape=[jax.ShapeDtypeStruct((B, T, HW), BF16),
                   jax.ShapeDtypeStruct((B, HEADS, LANES, LANES), F32)],
        compiler_params=_cparams("arbitrary"), name="gdn",
    )(three(q), three(k), three(v), gbeta.reshape(B, T, LANES), grow, zab.reshape(B, T, ZAB_WIDTH), go_rows)


def _softplus2(z):
    return jnp.maximum(z, jnp.log2(1.0 + jnp.exp2(jnp.minimum(z, 126.0))))


def _sb_blocks(qs, k_ref, v_ref, biases, jj, carries, accs, *, q_start=None):
    tq, tk = SB_TQ, SB_TK
    ks = pl.ds(pl.multiple_of(jj * tk, tk), tk)
    r = lax.broadcasted_iota(jnp.int32, (tk, tk), 0)
    s = lax.broadcasted_iota(jnp.int32, (tk, tk), 1)
    later = jnp.where(r > s, 1.0, 0.0).astype(BF16)
    if q_start is not None:
        mask = (jj * tk + lax.broadcasted_iota(jnp.int32, (tq, tk), 1)
                < q_start + lax.broadcasted_iota(jnp.int32, (tq, tk), 0))
    zs = [_dot_nt(q, k_ref[n, ks, :].astype(BF16)) + bias for n, (q, bias) in enumerate(zip(qs, biases))]
    sps = [_softplus2(z) for z in zs]
    if q_start is not None:
        sps = [jnp.where(mask, sp, 0.0) for sp in sps]
    tails = [_dot(sp.astype(BF16), later) for sp in sps]
    ws = [jnp.exp2((z - sp) - tail - carry) for z, sp, tail, carry in zip(zs, sps, tails, carries)]
    if q_start is not None:
        ws = [jnp.where(mask, w, 0.0) for w in ws]
    accs = [acc + _dot(w.astype(BF16), v_ref[n, ks, :].astype(BF16)) for n, (acc, w) in enumerate(zip(accs, ws))]
    carries = [carry + (tail[:, :1] + sp[:, :1]) for carry, tail, sp in zip(carries, tails, sps)]
    return tuple(carries), tuple(accs)


def _sb_kernel(q_ref, k_ref, v_ref, bias_ref, o_ref):
    i = pl.program_id(2)
    streams = range(SB_STREAMS)
    per_q = SB_TQ // SB_TK
    qs = [(q_ref[:, n * LANES:(n + 1) * LANES] * (LANES ** -0.5 * LOG2E)).astype(BF16) for n in streams]
    biases = [bias_ref[n] for n in streams]
    state = (tuple(jnp.zeros((SB_TQ, 1), F32) for _ in streams),
             tuple(jnp.zeros((SB_TQ, LANES), F32) for _ in streams))
    for d in range(per_q):
        state = _sb_blocks(qs, k_ref, v_ref, biases, (i + 1) * per_q - 1 - d, *state, q_start=i * SB_TQ)

    def body(step, state):
        return _sb_blocks(qs, k_ref, v_ref, biases, i * per_q - 1 - step, *state)

    carries, accs = lax.fori_loop(0, i * per_q, body, state)
    for n in streams:
        o_ref[:, n * LANES:(n + 1) * LANES] = accs[n].astype(o_ref.dtype)


def _sb_attention(qc, kc, vc, bias2_keys, B, T, l):
    assert SB_TQ % SB_TK == 0 and T % SB_TQ == 0 and HEADS % SB_STREAMS == 0
    width = SB_STREAMS * LANES
    kv = pl.BlockSpec((None, SB_STREAMS, T, LANES), lambda b, g, i: (b, g, 0, 0))
    qo = pl.BlockSpec((None, SB_TQ, width), lambda b, g, i: (b, i, g))
    return pl.pallas_call(
        _sb_kernel, grid=(B, HEADS // SB_STREAMS, T // SB_TQ),
        in_specs=[qo, kv, kv, pl.BlockSpec((None, SB_STREAMS, 1, SB_TK), lambda b, g, i: (l, g, 0, 0))],
        out_specs=qo, out_shape=jax.ShapeDtypeStruct((B, T, HW), BF16),
        compiler_params=_cparams("parallel", "parallel", "parallel"), name="sb_attention",
    )(qc.reshape(B, T, HW), kc, vc, bias2_keys)


def _sb_decode_kernel(pt_ref, q_ref, bias_ref, *rest):
    del pt_ref
    k_refs = rest[:PAGES_PER_STEP]
    v_refs = rest[PAGES_PER_STEP:2 * PAGES_PER_STEP]
    o_ref, carry_ref = rest[2 * PAGES_PER_STEP:]
    j = pl.program_id(1)

    @pl.when(j == 0)
    def _():
        o_ref[...] = jnp.zeros_like(o_ref)
        carry_ref[...] = jnp.zeros_like(carry_ref)

    qb = (q_ref[...] * (LANES ** -0.5 * LOG2E)).astype(BF16)
    bias = bias_ref[...]
    P = k_refs[0].shape[1]
    head = lax.broadcasted_iota(jnp.int32, (SUBLANES, LANES), 0)
    r = lax.broadcasted_iota(jnp.int32, (P, P), 0)
    s = lax.broadcasted_iota(jnp.int32, (P, P), 1)
    later = jnp.where(r > s, 1.0, 0.0).astype(BF16)
    pages = range(PAGES_PER_STEP)
    zhs = [[_dot_nt(qb, k_refs[p][h].astype(BF16)) for h in range(HEADS)] for p in pages]
    zs = []
    for p in pages:
        z = zhs[p][0]
        for h in range(1, HEADS):
            z = jnp.where(head == h, zhs[p][h], z)
        zs.append(z + bias)
    sps = [_softplus2(z) for z in zs]
    tails = [_dot(sp.astype(BF16), later) for sp in sps]
    carries = [carry_ref[...]]
    for p in pages:
        carries.append(carries[-1] + (tails[p][:, :1] + sps[p][:, :1]))
    ws = [jnp.exp2((zs[p] - sps[p]) - tails[p] - carries[p]).astype(BF16) for p in pages]
    ohs = [[_dot(ws[p], v_refs[p][h].astype(BF16)) for h in range(HEADS)] for p in pages]
    acc = o_ref[...]
    for h in range(HEADS):
        acc = acc + jnp.where(head == h, sum(ohs[p][h] for p in pages), 0.0)
    o_ref[...] = acc
    carry_ref[...] = carries[-1]


def _sb_decode(qc, cache_k, cache_v, page_table, bias2_lanes, l):
    Bs, NP = page_table.shape
    P = cache_k.shape[3]
    assert NP % PAGES_PER_STEP == 0 and P == LANES
    pt = page_table.reshape(-1)
    q = jnp.pad(qc.reshape(Bs, HEADS, LANES), ((0, 0), (0, SUBLANES - HEADS), (0, 0)))

    def page_spec(p):
        def index(b, j, pt_ref):
            return (l, pt_ref[b * NP + (NP - 1 - (j * PAGES_PER_STEP + p))], 0, 0, 0)
        return pl.BlockSpec((None, None, HEADS, P, LANES), index)

    row = pl.BlockSpec((None, SUBLANES, LANES), lambda b, j, pt_ref: (b, 0, 0))
    pages = [page_spec(p) for p in range(PAGES_PER_STEP)]
    out = pl.pallas_call(
        _sb_decode_kernel,
        grid_spec=pltpu.PrefetchScalarGridSpec(
            num_scalar_prefetch=1, grid=(Bs, NP // PAGES_PER_STEP),
            in_specs=[row, pl.BlockSpec((None, SUBLANES, LANES), lambda b, j, pt_ref: (l, 0, 0))] + pages + pages,
            out_specs=row, scratch_shapes=[pltpu.VMEM((SUBLANES, LANES), F32)]),
        out_shape=jax.ShapeDtypeStruct((Bs, SUBLANES, LANES), F32),
        compiler_params=_cparams("parallel", "arbitrary"), name="sb_decode",
    )(pt, q, bias2_lanes, *([cache_k] * PAGES_PER_STEP), *([cache_v] * PAGES_PER_STEP))
    return out[:, :HEADS].reshape(Bs, HW)


def _columns(x):
    pad = jnp.zeros((LANES - x.shape[0], LANES), F32)
    return jnp.concatenate([x, pad], axis=0).T


def _sample_mix_kernel(gu_ref, vn_ref, ws_ref, bs_ref, x_ref, hist_ref, cw_ref, ab_ref, alog_ref, dtb_ref,
                       z_ref, s_ref, go_ref, ya_ref, yb_ref, sn_ref, o_scr):
    Bs = gu_ref.shape[0]
    ya_ref[...] = gu_ref[...] * (ws_ref[...] * vn_ref[...] + bs_ref[...])
    y = cw_ref[CONV_W - 1:CONV_W, :] * x_ref[...]
    for s in range(CONV_W - 1):
        y = y + cw_ref[s:s + 1, :] * hist_ref[s]
    y = _silu(y)
    ab = ab_ref[...]
    decay = jnp.exp(-jnp.exp(alog_ref[...]) * _softplus(ab + dtb_ref[...]))
    beta = _sigmoid(ab)
    for h in range(HEADS):
        hs = slice(h * LANES, (h + 1) * LANES)
        q = _l2n_rows(y[:, hs]) * (LANES ** -0.5)
        k = _l2n_rows(y[:, HW + h * LANES:HW + (h + 1) * LANES])
        v = y[:, 2 * HW + h * LANES:2 * HW + (h + 1) * LANES]
        qcols, kcols = _columns(q), _columns(k)
        for b in range(Bs):
            a = decay[b:b + 1, h:h + 1]
            bt = beta[b:b + 1, HEADS + h:HEADS + h + 1]
            state = s_ref[b, h]
            kcol = kcols[:, b:b + 1]
            ks = jnp.sum(kcol * state, axis=0, keepdims=True)
            u = bt * (v[b:b + 1, :] - a * ks)
            new_state = a * state + kcol * u
            sn_ref[b, h] = new_state
            o_scr[b:b + 1, hs] = jnp.sum(qcols[:, b:b + 1] * new_state, axis=0, keepdims=True)
    for h in range(HEADS):
        hs = slice(h * LANES, (h + 1) * LANES)
        yb_ref[:, hs] = _rms_rows(o_scr[:, hs]) * go_ref[...] * z_ref[:, hs]


def _sample_mix(gu, vn, ws_rows, bs_rows, bqkv, hist, conv_w, zab, alog_rows, dtb_rows, state, go_rows, l):
    Bs = gu.shape[0]
    full = lambda shape: pl.BlockSpec(shape, lambda i: (0,) * len(shape))
    par = lambda shape: pl.BlockSpec((None,) + shape, lambda i: (l,) + (0,) * len(shape))
    ab, zsilu = zab, zab
    return pl.pallas_call(
        _sample_mix_kernel, grid=(1,),
        in_specs=[full((Bs, A_WIDTH)), full((Bs, A_WIDTH)), par((1, A_WIDTH)), par((1, A_WIDTH)),
                  full((Bs, 3 * HW)), par((CONV_W - 1, Bs, 3 * HW)), par((CONV_W, 3 * HW)),
                  pl.BlockSpec((Bs, LANES), lambda i: (0, HW // LANES)), par((1, LANES)), par((1, LANES)),
                  full((Bs, HW)), par((Bs, HEADS, LANES, LANES)), par((1, LANES))],
        out_specs=[full((Bs, A_WIDTH)), full((Bs, HW)), full((Bs, HEADS, LANES, LANES))],
        out_shape=[jax.ShapeDtypeStruct((Bs, A_WIDTH), F32), jax.ShapeDtypeStruct((Bs, HW), F32),
                   jax.ShapeDtypeStruct((Bs, HEADS, LANES, LANES), F32)],
        scratch_shapes=[pltpu.VMEM((Bs, HW), F32)],
        compiler_params=_cparams("arbitrary"), name="sample_mix",
    )(gu, vn, ws_rows, bs_rows, bqkv, hist, conv_w, ab, alog_rows, dtb_rows, zsilu, state, go_rows)


def _merge_kernel(ya_ref, yb_ref, yc_ref, ga_ref, gb_ref, gc_ref, yas_ref, ybs_ref, ycs_ref, gas_ref, gbs_ref, gcs_ref,
                  wa_ref, wb_ref, wc_ref, o_ref, os_ref):
    def merged(ya, yb, yc, ga, gb, gc):
        m = (ga[...] * _dot(ya[...].astype(BF16), wa_ref[...]) + gb[...] * _dot(yb[...].astype(BF16), wb_ref[...])
             + gc[...] * _dot(yc[...].astype(BF16), wc_ref[...]))
        return m.astype(o_ref.dtype)

    o_ref[...] = merged(ya_ref, yb_ref, yc_ref, ga_ref, gb_ref, gc_ref)

    @pl.when(_last_row_step(1))
    def _():
        os_ref[...] = merged(yas_ref, ybs_ref, ycs_ref, gas_ref, gbs_ref, gcs_ref)


def _merge(ys, ys_s, gates, gates_s, wa, wb, wc, l):
    R = ys[0].shape[0]
    Bs = ys_s[0].shape[0]
    D = wa.shape[-1]
    tm = _tile(R, ROW_TILES)
    tn = _tile(D, (1024, 512, 256, LANES))
    nj = D // tn
    rows = lambda width: pl.BlockSpec((tm, width), lambda j, i: (i, 0))
    srows = lambda width: pl.BlockSpec((Bs, width), lambda j, i: (0, 0))
    gate = lambda k: pl.BlockSpec((tm, tn), lambda j, i: (i, k * nj + j))
    sgate = lambda k: pl.BlockSpec((Bs, tn), lambda j, i: (0, k * nj + j))
    wspec = lambda width: pl.BlockSpec((None, width, tn), lambda j, i: (l, 0, j))
    return pl.pallas_call(
        _merge_kernel, grid=(nj, R // tm),
        in_specs=[rows(A_WIDTH), rows(HW), rows(HW), gate(0), gate(1), gate(2),
                  srows(A_WIDTH), srows(HW), srows(HW), sgate(0), sgate(1), sgate(2),
                  wspec(A_WIDTH), wspec(HW), wspec(HW)],
        out_specs=[pl.BlockSpec((tm, tn), lambda j, i: (i, j)), pl.BlockSpec((Bs, tn), lambda j, i: (0, j))],
        out_shape=[jax.ShapeDtypeStruct((R, D), BF16), jax.ShapeDtypeStruct((Bs, D), BF16)],
        compiler_params=_cparams("parallel", "arbitrary"), name="merge",
    )(*ys, gates, gates, gates, *ys_s, gates_s, gates_s, gates_s, wa, wb, wc)


def _outproj_kernel(m_ref, x_ref, ms_ref, xs_ref, w_ref, g_ref, x1_ref, h2_ref, x1s_ref, h2s_ref):
    def project(m, x, x1_out, h2_out):
        x1 = x + _dot(m, w_ref[...])
        x1_out[...] = x1
        h2_out[...] = (_rms_rows(x1) * g_ref[...]).astype(h2_out.dtype)

    project(m_ref[...], x_ref[...], x1_ref, h2_ref)

    @pl.when(_last_row_step(0))
    def _():
        project(ms_ref[...], xs_ref[...], x1s_ref, h2s_ref)


def _outproj(m, x, m_s, x_s, w_out, gains, l):
    R, D = x.shape
    Bs = x_s.shape[0]
    tm = _tile(R, ROW_TILES[1:])
    rows = pl.BlockSpec((tm, D), lambda i: (i, 0))
    srows = pl.BlockSpec((Bs, D), lambda i: (0, 0))
    return pl.pallas_call(
        _outproj_kernel, grid=(R // tm,),
        in_specs=[rows, rows, srows, srows, pl.BlockSpec((None, D, D), lambda i: (l, 0, 0)),
                  pl.BlockSpec((None, 1, D), lambda i: (l, 0, 0))],
        out_specs=[rows, rows, srows, srows],
        out_shape=[jax.ShapeDtypeStruct((R, D), F32), jax.ShapeDtypeStruct((R, D), BF16),
                   jax.ShapeDtypeStruct((Bs, D), F32), jax.ShapeDtypeStruct((Bs, D), BF16)],
        compiler_params=_cparams("arbitrary"), name="outproj")(m, x, m_s, x_s, w_out, gains)


def _ffn_up_kernel(h_ref, hs_ref, wg_ref, wu_ref, o_ref, os_ref, wgb_ref, wub_ref):
    @pl.when(pl.program_id(1) == 0)
    def _():
        wgb_ref[...] = wg_ref[...].astype(BF16)
        wub_ref[...] = wu_ref[...].astype(BF16)

    def swiglu(h):
        return _silu(_dot(h, wgb_ref[...])) * _dot(h, wub_ref[...])

    o_ref[...] = swiglu(h_ref[...]).astype(o_ref.dtype)

    @pl.when(_last_row_step(1))
    def _():
        os_ref[...] = swiglu(hs_ref[...]).astype(os_ref.dtype)


def _ffn_up(h2, h2_s, w_gate, w_up, l):
    R, D = h2.shape
    Bs = h2_s.shape[0]
    F = w_gate.shape[-1]
    tm = _tile(R, ROW_TILES)
    tn = _tile(F, (512, 256, LANES))
    wspec = pl.BlockSpec((None, D, tn), lambda j, i: (l, 0, j))
    return pl.pallas_call(
        _ffn_up_kernel, grid=(F // tn, R // tm),
        in_specs=[pl.BlockSpec((tm, D), lambda j, i: (i, 0)), pl.BlockSpec((Bs, D), lambda j, i: (0, 0)),
                  wspec, wspec],
        out_specs=[pl.BlockSpec((tm, tn), lambda j, i: (i, j)), pl.BlockSpec((Bs, tn), lambda j, i: (0, j))],
        out_shape=[jax.ShapeDtypeStruct((R, F), BF16), jax.ShapeDtypeStruct((Bs, F), BF16)],
        scratch_shapes=[pltpu.VMEM((D, tn), BF16), pltpu.VMEM((D, tn), BF16)],
        compiler_params=_cparams("arbitrary", "arbitrary"), name="ffn_up")(h2, h2_s, w_gate, w_up)


def _ffn_down_kernel(a_ref, x_ref, as_ref, xs_ref, w_ref, o_ref, os_ref, wb_ref):
    @pl.when(pl.program_id(1) == 0)
    def _():
        wb_ref[...] = w_ref[...].astype(BF16)

    o_ref[...] = x_ref[...] + _dot(a_ref[...], wb_ref[...])

    @pl.when(_last_row_step(1))
    def _():
        os_ref[...] = xs_ref[...] + _dot(as_ref[...], wb_ref[...])


def _ffn_down(act, x1, act_s, x1_s, w_down, l):
    R, F = act.shape
    Bs = act_s.shape[0]
    D = x1.shape[-1]
    tm = _tile(R, ROW_TILES[1:])
    tn = _tile(D, (512, 256, LANES))
    tile = pl.BlockSpec((tm, tn), lambda j, i: (i, j))
    stile = pl.BlockSpec((Bs, tn), lambda j, i: (0, j))
    return pl.pallas_call(
        _ffn_down_kernel, grid=(D // tn, R // tm),
        in_specs=[pl.BlockSpec((tm, F), lambda j, i: (i, 0)), tile,
                  pl.BlockSpec((Bs, F), lambda j, i: (0, 0)), stile,
                  pl.BlockSpec((None, F, tn), lambda j, i: (l, 0, j))],
        out_specs=[tile, stile],
        out_shape=[jax.ShapeDtypeStruct((R, D), F32), jax.ShapeDtypeStruct((Bs, D), F32)],
        scratch_shapes=[pltpu.VMEM((F, tn), BF16)],
        compiler_params=_cparams("arbitrary", "arbitrary"), name="ffn_down")(act, x1, act_s, x1_s, w_down)


def _prepare_params(d_model, norm_mix, w_in, a_vnorm, a_ws, a_bs, dn_conv, dn_a_log, dn_dt_bias, dn_onorm,
                    sb_qnorm, sb_knorm, sb_bias, w_branch, w_out, norm_ffn, w_gate, w_up, w_down):
    depth = w_in.shape[0]
    sizes = (A_WIDTH, A_WIDTH, 3 * HW, HW + 2 * HEADS, HW, HW, HW, 3 * d_model)
    names = ("a_u", "a_v", "b_qkv", "b_zab", "c_q", "c_k", "c_v", "gates")
    p, lo = {}, 0
    for name, size in zip(names, sizes, strict=True):
        p["w_" + name] = w_in[:, :, lo:lo + size].astype(BF16)
        lo += size
    p["w_b_zab"] = jnp.pad(p["w_b_zab"], ((0, 0), (0, 0), (0, ZAB_WIDTH - HW - 2 * HEADS)))
    pad_heads = lambda a: jnp.pad(a, ((0, 0), (0, LANES - HEADS)))[:, None, :]
    tile_heads = lambda a: jnp.tile(a, (1, HEADS))[:, None, :]
    p.update(
        norm_mix=norm_mix[:, None, :], norm_ffn=norm_ffn[:, None, :],
        a_vnorm=a_vnorm.reshape(depth, 1, A_WIDTH), a_ws=a_ws,
        a_bs_rows=jnp.repeat(jnp.swapaxes(a_bs, 1, 2), LANES, axis=2),
        a_ws0=jnp.repeat(a_ws[:, :, 0, 0], LANES, axis=1)[:, None, :],
        a_bs0=jnp.repeat(a_bs[:, :, 0], LANES, axis=1)[:, None, :],
        dn_conv=dn_conv, alog=pad_heads(dn_a_log), dtb=pad_heads(dn_dt_bias),
        dn_onorm=dn_onorm[:, None, :], sb_qnorm=tile_heads(sb_qnorm), sb_knorm=tile_heads(sb_knorm),
        sb_bias2_keys=jnp.broadcast_to((sb_bias * LOG2E)[:, :, None, None], sb_bias.shape + (1, SB_TK)),
        sb_bias2_lanes=jnp.broadcast_to(jnp.pad(sb_bias * LOG2E, ((0, 0), (0, SUBLANES - HEADS)))[:, :, None],
                                        (depth, SUBLANES, LANES)),
        wb_a=w_branch[:, :A_WIDTH].astype(BF16), wb_b=w_branch[:, A_WIDTH:A_WIDTH + HW].astype(BF16),
        wb_c=w_branch[:, A_WIDTH + HW:].astype(BF16), w_out=w_out.astype(BF16),
        w_gate=w_gate, w_up=w_up, w_down=w_down)
    return p


def _project(x, xs, p, l, B, T):
    h, hs = _rms_cast(x, xs, p["norm_mix"], l)
    epi = lambda **kw: functools.partial(_epilogue, **kw)
    calls = dict(
        gu=_proj(h, hs, p["w_a_u"], l, epi(act=jax.nn.gelu)),
        vn=_proj(h, hs, p["w_a_v"], l, epi(act=jax.nn.gelu, group_rms=True), p["a_vnorm"], out_dtypes=(BF16,)),
        bqkv=_proj(h, hs, p["w_b_qkv"], l, epi()),
        zab=_proj(h, hs, p["w_b_zab"], l, epi(act=_silu, act_cols=HW)),
        qc=_proj(h, hs, p["w_c_q"], l, epi(group_rms=True), p["sb_qnorm"]),
        kc=_proj(h, hs, p["w_c_k"], l, epi(group_rms=True), p["sb_knorm"], head_major=(B, T), out_dtypes=(F32, BF16)),
        vc=_proj(h, hs, p["w_c_v"], l, epi(), head_major=(B, T), out_dtypes=(F32, BF16)),
        gates=_proj(h, hs, p["w_gates"], l, epi(act=_sigmoid), out_dtypes=(BF16,)))
    prompt = {k: v[0] for k, v in calls.items()}
    prompt.update(kc_bf16=calls["kc"][1], vc_bf16=calls["vc"][1])
    return prompt, {k: v[-1] for k, v in calls.items()}


def _layer(x, xs, p, l, B, T, cache_k, cache_v, page_table, state_delta, state_conv):
    Bs = xs.shape[0]
    s, ss = _project(x, xs, p, l, B, T)
    ya = _mixer_a(s["gu"], s["vn"], p["a_ws"], p["a_bs_rows"], l)
    q, k, v, gbeta, grow = _gdn_prep(s["bqkv"], s["zab"], p["dn_conv"], p["alog"], p["dtb"], B, T, l)
    yb, state = _gdn(q, k, v, gbeta, grow, s["zab"], p["dn_onorm"], B, T, l)
    yc = _sb_attention(s["qc"], s["kc_bf16"], s["vc_bf16"], p["sb_bias2_keys"], B, T, l)
    hist = jnp.swapaxes(state_conv, 1, 2)
    ya_s, yb_s, state_s = _sample_mix(ss["gu"], ss["vn"], p["a_ws0"], p["a_bs0"], ss["bqkv"], hist, p["dn_conv"],
                                      ss["zab"], p["alog"], p["dtb"], state_delta, p["dn_onorm"], l)
    yc_s = _sb_decode(ss["qc"], cache_k, cache_v, page_table, p["sb_bias2_lanes"], l)
    m, m_s = _merge((ya, yb.reshape(B * T, HW), yc.reshape(B * T, HW)), (ya_s, yb_s, yc_s), s["gates"], ss["gates"],
                    p["wb_a"], p["wb_b"], p["wb_c"], l)
    x1, h2, x1_s, h2_s = _outproj(m, x, m_s, xs, p["w_out"], p["norm_ffn"], l)
    act, act_s = _ffn_up(h2, h2_s, p["w_gate"], p["w_up"], l)
    y, y_s = _ffn_down(act, x1, act_s, x1_s, p["w_down"], l)
    conv_tail = s["bqkv"].reshape(B, T, 3 * HW)[:, T - (CONV_W - 1):]
    conv_s = jnp.concatenate([state_conv[l][:, 1:], ss["bqkv"][:, None, :]], axis=1)
    prompt_outs = (s["kc"], s["vc"], state, conv_tail)
    sample_outs = (ss["kc"].reshape(Bs, 1, HEADS, LANES), ss["vc"].reshape(Bs, 1, HEADS, LANES), state_s, conv_s,
                   ss["vn"][:, None, :])
    return y, y_s, prompt_outs, sample_outs


def kernel(x_prompt, x_sample, cache_k, cache_v, state_delta, state_conv, page_table, norm_mix, w_in, a_vnorm, a_ws, a_bs, dn_conv, dn_a_log, dn_dt_bias, dn_onorm, sb_qnorm, sb_knorm, sb_bias, w_branch, w_out, norm_ffn, w_gate, w_up, w_down):
    B, T, D = x_prompt.shape
    Bs = x_sample.shape[0]
    depth = w_in.shape[0]
    assert x_sample.shape[1] == 1 and T % SB_TQ == 0 and T % GDN_CHUNK == 0
    p = _prepare_params(D, norm_mix, w_in, a_vnorm, a_ws, a_bs, dn_conv, dn_a_log, dn_dt_bias, dn_onorm,
                        sb_qnorm, sb_knorm, sb_bias, w_branch, w_out, norm_ffn, w_gate, w_up, w_down)
    cache_k = jnp.swapaxes(cache_k, 2, 3)
    cache_v = jnp.swapaxes(cache_v, 2, 3)
    yp = x_prompt.reshape(B * T, D)
    ys = x_sample.reshape(Bs, D)
    prompt_outs, sample_outs = [], []
    for l in range(depth):
        yp, ys, outs, outs_s = _layer(yp, ys, p, l, B, T, cache_k, cache_v, page_table, state_delta, state_conv)
        prompt_outs.append(outs)
        sample_outs.append(outs_s)
    stack = lambda outs, i: jnp.stack([o[i] for o in outs])
    by_token = lambda a: jnp.swapaxes(a, 2, 3)
    return (yp.reshape(B, T, D), ys.reshape(Bs, 1, D),
            by_token(stack(prompt_outs, 0)), by_token(stack(prompt_outs, 1)), stack(prompt_outs, 2), stack(prompt_outs, 3),
            stack(sample_outs, 0), stack(sample_outs, 1), stack(sample_outs, 2), stack(sample_outs, 3),
            stack(sample_outs, 4))
```

```python
import functools

import jax
import jax.numpy as jnp
from jax import lax
from jax.experimental import pallas as pl
from jax.experimental.pallas import tpu as pltpu

F32 = jnp.float32
BF16 = jnp.bfloat16
EPS = 1e-6
LANES = 128
SUBLANES = 8
A_GROUPS = 4
A_WIDTH = A_GROUPS * LANES
A_CHUNK = 128
HEADS = 6
HW = HEADS * LANES
CONV_W = 4
ZAB_WIDTH = HW + LANES
GDN_CHUNK = 128
SB_TQ = 256
SB_TK = 256
SB_STREAMS = 6
PAGES_PER_STEP = 16
ROW_TILES = (1024, 512, 256, 128, SUBLANES)
LOG2E = 1.4426950408889634
VMEM_LIMIT_BYTES = 56 * 1024 * 1024


def _tile(n, candidates):
    for c in candidates:
        if n % c == 0:
            return c
    return n


def _cparams(*sem):
    return pltpu.CompilerParams(dimension_semantics=sem, vmem_limit_bytes=VMEM_LIMIT_BYTES)


def _dot(a, b):
    return jnp.dot(a, b, preferred_element_type=F32)


def _dot_nt(a, b):
    return lax.dot_general(a, b, (((1,), (1,)), ((), ())), preferred_element_type=F32)


def _sigmoid(x):
    return 1.0 / (1.0 + jnp.exp(-x))


def _silu(x):
    return x * _sigmoid(x)


def _softplus(x):
    return jnp.maximum(x, 0.0) + jnp.log(1.0 + jnp.exp(-jnp.abs(x)))


def _rms_rows(x):
    return x * lax.rsqrt(jnp.mean(x * x, axis=-1, keepdims=True) + EPS)


def _l2n_rows(x):
    return x * lax.rsqrt(jnp.sum(x * x, axis=-1, keepdims=True) + EPS)


def _last_row_step(axis):
    return pl.program_id(axis) == pl.num_programs(axis) - 1


def _rms_cast_kernel(x_ref, xs_ref, g_ref, o_ref, os_ref):
    o_ref[...] = (_rms_rows(x_ref[...]) * g_ref[...]).astype(o_ref.dtype)

    @pl.when(_last_row_step(0))
    def _():
        os_ref[...] = (_rms_rows(xs_ref[...]) * g_ref[...]).astype(os_ref.dtype)


def _rms_cast(x, xs, gains, l):
    R, D = x.shape
    Bs = xs.shape[0]
    tr = _tile(R, ROW_TILES)
    rows = pl.BlockSpec((tr, D), lambda i: (i, 0))
    srows = pl.BlockSpec((Bs, D), lambda i: (0, 0))
    return pl.pallas_call(
        _rms_cast_kernel, grid=(R // tr,),
        in_specs=[rows, srows, pl.BlockSpec((None, 1, D), lambda i: (l, 0, 0))],
        out_specs=[rows, srows],
        out_shape=[jax.ShapeDtypeStruct((R, D), BF16), jax.ShapeDtypeStruct((Bs, D), BF16)],
        compiler_params=_cparams("arbitrary"), name="rms_cast")(x, xs, gains)


def _store_group(o_ref, c, val, head_major):
    if head_major:
        o_ref[c] = val.astype(o_ref.dtype)
    else:
        o_ref[:, c * LANES:(c + 1) * LANES] = val.astype(o_ref.dtype)


def _epilogue(acc, gain_ref, o_refs, head_major, *, act=None, act_cols=None, group_rms=False):
    for c in range(acc.shape[1] // LANES):
        cs = slice(c * LANES, (c + 1) * LANES)
        blk = acc[:, cs]
        if act is not None and (act_cols is None or c * LANES < act_cols):
            blk = act(blk)
        if group_rms:
            blk = _rms_rows(blk) * gain_ref[:, cs]
        for o_ref in o_refs:
            _store_group(o_ref, c, blk, head_major)


def _proj_kernel(h_ref, hs_ref, w_ref, *rest, epilogue, head_major, has_gain, n_out):
    gain_ref = rest[0] if has_gain else None
    o_refs, os_ref = rest[-1 - n_out:-1], rest[-1]
    w = w_ref[...]
    epilogue(_dot(h_ref[...], w), gain_ref, o_refs, head_major)

    @pl.when(_last_row_step(1))
    def _():
        epilogue(_dot(hs_ref[...], w), gain_ref, (os_ref,), False)


def _proj(h, hs, w, l, epilogue, gain=None, head_major=None, out_dtypes=(F32,), layer_stack=None):
    R, K = h.shape
    Bs = hs.shape[0]
    N = w.shape[-1]
    tm = _tile(head_major[1] if head_major else R, ROW_TILES)
    tn = N if head_major else _tile(N, (1024, 896, 768, 512, 384, 256, LANES))
    in_specs = [pl.BlockSpec((tm, K), lambda j, i: (i, 0)),
                pl.BlockSpec((Bs, K), lambda j, i: (0, 0)),
                pl.BlockSpec((None, K, tn), lambda j, i: (l, 0, j))]
    args = [h, hs, w]
    if gain is not None:
        in_specs.append(pl.BlockSpec((None, 1, tn), lambda j, i: (l, 0, j)))
        args.append(gain)
    if head_major:
        B, T = head_major
        nb = T // tm
        out_spec = pl.BlockSpec((None, N // LANES, tm, LANES), lambda j, i: (i // nb, 0, i % nb, 0))
        shape = (B, N // LANES, T, LANES)
    else:
        out_spec = pl.BlockSpec((tm, tn), lambda j, i: (i, j))
        shape = (R, N)
    out_specs = [out_spec] * len(out_dtypes)
    out_shapes = [jax.ShapeDtypeStruct(shape, dt) for dt in out_dtypes]
    aliases = {}
    if layer_stack:
        depth, buffer = layer_stack
        out_specs[0] = pl.BlockSpec((None, None, N // LANES, tm, LANES), lambda j, i: (l, i // nb, 0, i % nb, 0))
        out_shapes[0] = jax.ShapeDtypeStruct((depth,) + shape, out_dtypes[0])
        if buffer is not None:
            aliases = {len(args): 0}
            in_specs.append(pl.BlockSpec(memory_space=pl.ANY))
            args.append(buffer)
    return pl.pallas_call(
        functools.partial(_proj_kernel, epilogue=epilogue, head_major=bool(head_major), has_gain=gain is not None,
                          n_out=len(out_dtypes)),
        grid=(N // tn, R // tm), in_specs=in_specs,
        out_specs=out_specs + [pl.BlockSpec((Bs, tn), lambda j, i: (0, j))],
        out_shape=out_shapes + [jax.ShapeDtypeStruct((Bs, N), F32)], input_output_aliases=aliases,
        compiler_params=_cparams("parallel", "arbitrary"), name="proj")(*args)


def _mixer_a_kernel(gu_ref, vn_ref, ws_ref, bs_ref, o_ref, *, chunks):
    t = lax.broadcasted_iota(jnp.int32, (A_CHUNK, A_CHUNK), 0)
    s = lax.broadcasted_iota(jnp.int32, (A_CHUNK, A_CHUNK), 1)
    for g in range(A_GROUPS):
        cs = slice(g * LANES, (g + 1) * LANES)
        w = jnp.where(t >= s, ws_ref[g], 0.0).astype(BF16)
        for c in range(chunks):
            rs = slice(c * A_CHUNK, (c + 1) * A_CHUNK)
            mixed = _dot(w, vn_ref[rs, cs].astype(BF16)) + bs_ref[:, cs]
            o_ref[rs, cs] = (gu_ref[rs, cs] * mixed).astype(o_ref.dtype)


def _mixer_a(gu, vn, a_ws, bs_rows, l):
    R = gu.shape[0]
    chunks = _tile(R // A_CHUNK, (4, 2, 1))
    tr = chunks * A_CHUNK
    return pl.pallas_call(
        functools.partial(_mixer_a_kernel, chunks=chunks), grid=(R // tr,),
        in_specs=[pl.BlockSpec((tr, A_WIDTH), lambda i: (i, 0)),
                  pl.BlockSpec((tr, A_WIDTH), lambda i: (i, 0)),
                  pl.BlockSpec((None, A_GROUPS, A_CHUNK, A_CHUNK), lambda i: (l, 0, 0, 0)),
                  pl.BlockSpec((None, A_CHUNK, A_WIDTH), lambda i: (l, 0, 0))],
        out_specs=pl.BlockSpec((tr, A_WIDTH), lambda i: (i, 0)),
        out_shape=jax.ShapeDtypeStruct((R, A_WIDTH), BF16),
        compiler_params=_cparams("parallel"), name="mixer_a")(gu, vn, a_ws, bs_rows)


def _split3_bf16(x):
    a = x.astype(BF16)
    r = x - a.astype(F32)
    b = r.astype(BF16)
    c = (r - b.astype(F32)).astype(BF16)
    return a, b, c


def _gdn_prep_kernel(x_ref, halo_ref, cw_ref, ab_ref, alog_ref, dtb_ref,
                     q_ref, k_ref, v_ref, gbeta_ref, grow_ref, xp_ref, *, tr):
    i = pl.program_id(1)
    xp_ref[0:SUBLANES, :] = jnp.where(i == 0, 0.0, halo_ref[...])
    xp_ref[SUBLANES:, :] = x_ref[...]
    y = cw_ref[CONV_W - 1:CONV_W, :] * x_ref[...]
    for s in range(1, CONV_W):
        y = y + cw_ref[CONV_W - 1 - s:CONV_W - s, :] * xp_ref[pl.ds(SUBLANES - s, tr), :]
    y = _silu(y)
    for h in range(HEADS):
        hs = slice(h * LANES, (h + 1) * LANES)
        q_ref[:, hs] = _l2n_rows(y[:, hs]) * (LANES ** -0.5)
        k_ref[:, hs] = _l2n_rows(y[:, HW + h * LANES:HW + (h + 1) * LANES])
    v_ref[...] = y[:, 2 * HW:]
    ab = ab_ref[...]
    g = -jnp.exp(alog_ref[...]) * _softplus(ab + dtb_ref[...])
    beta = _sigmoid(ab)
    r = lax.broadcasted_iota(jnp.int32, (tr, tr), 0)
    c = lax.broadcasted_iota(jnp.int32, (tr, tr), 1)
    tri = jnp.where((r >= c) & (r // GDN_CHUNK == c // GDN_CHUNK), 1.0, 0.0).astype(BF16)
    gcum = sum(_dot(tri, part) for part in _split3_bf16(g))
    lane = lax.broadcasted_iota(jnp.int32, (tr, LANES), 1)
    gbeta_ref[...] = jnp.where(lane < HEADS, gcum, beta)
    gt = gcum.T
    for h in range(HEADS):
        grow_ref[h] = gt[h:h + 1, :]


def _gdn_prep(bqkv, zab, conv_w, alog_rows, dtb_rows, B, T, l):
    tr = _tile(T, (256, GDN_CHUNK))
    nb = T // tr
    row = lambda b, i: (b * nb + i, 0)
    ab_group = lambda b, i: (b * nb + i, HW // LANES)
    halo = lambda b, i: (jnp.maximum((b * T + i * tr) // SUBLANES - 1, 0), 0)
    par = lambda b, i: (l, 0, 0)
    R = B * T
    wide = jax.ShapeDtypeStruct((R, HW), F32)
    return pl.pallas_call(
        functools.partial(_gdn_prep_kernel, tr=tr), grid=(B, nb),
        in_specs=[pl.BlockSpec((tr, 3 * HW), row),
                  pl.BlockSpec((SUBLANES, 3 * HW), halo),
                  pl.BlockSpec((None, CONV_W, 3 * HW), par),
                  pl.BlockSpec((tr, LANES), ab_group),
                  pl.BlockSpec((None, 1, LANES), par),
                  pl.BlockSpec((None, 1, LANES), par)],
        out_specs=[pl.BlockSpec((tr, HW), row)] * 3 + [pl.BlockSpec((tr, LANES), row)]
        + [pl.BlockSpec((None, HEADS, 1, tr), lambda b, i: (b, 0, 0, i))],
        out_shape=[wide] * 3 + [jax.ShapeDtypeStruct((R, LANES), F32), jax.ShapeDtypeStruct((B, HEADS, 1, T), F32)],
        scratch_shapes=[pltpu.VMEM((tr + SUBLANES, 3 * HW), F32)],
        compiler_params=_cparams("parallel", "parallel"), name="gdn_prep",
    )(bqkv, bqkv, conv_w, zab, alog_rows, dtb_rows)


INV_BASE = 16


def _unit_lower_inverses_minus_eye(ns, r, c):
    L = ns[0].shape[0]
    shift = INV_BASE.bit_length() - 1
    same_block = (r >> shift) == (c >> shift)
    xos = [jnp.where(same_block, n, 0.0) for n in ns]
    nbs = [x.astype(BF16) for x in xos]
    ps = [_dot(nb, nb) for nb in nbs]
    for _ in range(shift - 2):
        boths = [_dot(jnp.concatenate([xo, p], axis=0).astype(BF16), p.astype(BF16)) for xo, p in zip(xos, ps)]
        xos = [xo + p + both[:L] for xo, p, both in zip(xos, ps, boths)]
        ps = [both[L:] for both in boths]
    xps = [_dot(xo.astype(BF16), p.astype(BF16)) for xo, p in zip(xos, ps)]
    xos = [xo + p + xp for xo, p, xp in zip(xos, ps, xps)]
    while (1 << shift) < L:
        below_left = ((r >> shift) ^ (c >> shift)) == 1
        offs = [jnp.where(below_left, n, 0.0) for n in ns]
        xbs = [xo.astype(BF16) for xo in xos]
        ys = [off + _dot(xb, off.astype(BF16)) for off, xb in zip(offs, xbs)]
        yxs = [_dot(y.astype(BF16), xb) for y, xb in zip(ys, xbs)]
        xos = [xo + y + yx for xo, y, yx in zip(xos, ys, yxs)]
        shift += 1
    return xos


def _gdn_chunk_updates(qs, ks, vs, gcs, bts, grows, states):
    L = qs[0].shape[0]
    r = lax.broadcasted_iota(jnp.int32, (L, L), 0)
    c = lax.broadcasted_iota(jnp.int32, (L, L), 1)
    decays = [jnp.where(r >= c, jnp.exp(gc - grow), 0.0) for gc, grow in zip(gcs, grows)]
    kqs = [_dot_nt(jnp.concatenate([k, q], axis=0).astype(BF16), k.astype(BF16)) for k, q in zip(ks, qs)]
    ns = [jnp.where(r > c, -(bt * kq[:L] * decay), 0.0) for bt, kq, decay in zip(bts, kqs, decays)]
    xos = _unit_lower_inverses_minus_eye(ns, r, c)
    gams = [jnp.exp(gc) for gc in gcs]
    bvs = [bt * v for bt, v in zip(bts, vs)]
    bgks = [bt * gam * k for bt, gam, k in zip(bts, gams, ks)]
    corrs = [_dot(xo.astype(BF16), jnp.concatenate([bv, bgk], axis=1).astype(BF16))
             for xo, bv, bgk in zip(xos, bvs, bgks)]
    wqs = [_dot(jnp.concatenate([bgk + corr[:, LANES:], gam * q], axis=0).astype(BF16), state.astype(BF16))
           for bgk, corr, gam, q, state in zip(bgks, corrs, gams, qs, states)]
    us = [(bv + corr[:, :LANES] - wq[:L]).astype(BF16) for bv, corr, wq in zip(bvs, corrs, wqs)]
    glasts = [gc[L - 1:L, :] for gc in gcs]
    kds = [(k * jnp.exp(glast - gc)).astype(BF16) for k, glast, gc in zip(ks, glasts, gcs)]
    outs = [wq[L:] + _dot((kq[L:] * decay).astype(BF16), u) for wq, kq, decay, u in zip(wqs, kqs, decays, us)]
    new_states = [jnp.exp(glast) * state + lax.dot_general(kd, u, (((0,), (0,)), ((), ())),
                                                           preferred_element_type=F32)
                  for glast, state, kd, u in zip(glasts, states, kds, us)]
    return outs, new_states


def _gdn_kernel(q_ref, k_ref, v_ref, gbeta_ref, grow_ref, z_ref, go_ref, y_ref, s_ref, *, B):
    @pl.when(pl.program_id(0) == 0)
    def _():
        s_ref[...] = jnp.zeros_like(s_ref)

    units = [(b, h, slice(h * LANES, (h + 1) * LANES)) for b in range(B) for h in range(HEADS)]
    L = q_ref.shape[1]
    across = lambda b, lane: jnp.broadcast_to(gbeta_ref[b, :, lane:lane + 1], (L, LANES))
    outs, new_states = _gdn_chunk_updates(
        [q_ref[b, :, hs] for b, h, hs in units], [k_ref[b, :, hs] for b, h, hs in units],
        [v_ref[b, :, hs] for b, h, hs in units], [across(b, h) for b, h, hs in units],
        [across(b, HEADS + h) for b, h, hs in units], [grow_ref[b, h] for b, h, hs in units],
        [s_ref[b, h] for b, h, hs in units])
    for (b, h, hs), out, new_state in zip(units, outs, new_states):
        s_ref[b, h] = new_state
        y_ref[b, :, hs] = (_rms_rows(out) * go_ref[...] * z_ref[b, :, hs]).astype(y_ref.dtype)


def _gdn(q, k, v, gbeta, grow, zab, go_rows, B, T, l):
    L = GDN_CHUNK
    blk = pl.BlockSpec((B, L, HW), lambda i: (0, i, 0))
    three = lambda a: a.reshape(B, T, HW)
    return pl.pallas_call(
        functools.partial(_gdn_kernel, B=B), grid=(T // L,),
        in_specs=[blk] * 3 + [pl.BlockSpec((B, L, LANES), lambda i: (0, i, 0)),
                              pl.BlockSpec((B, HEADS, 1, L), lambda i: (0, 0, 0, i)), blk,
                              pl.BlockSpec((None, 1, LANES), lambda i: (l, 0, 0))],
        out_specs=[blk, pl.BlockSpec((B, HEADS, LANES, LANES), lambda i: (0, 0, 0, 0))],
        out_shape=[jax.ShapeDtypeStruct((B, T, HW), BF16),
                   jax.ShapeDtypeStruct((B, HEADS, LANES, LANES), F32)],
        compiler_params=_cparams("arbitrary"), name="gdn",
    )(three(q), three(k), three(v), gbeta.reshape(B, T, LANES), grow, zab.reshape(B, T, ZAB_WIDTH), go_rows)


def _softplus2(z):
    return jnp.maximum(z, jnp.log2(1.0 + jnp.exp2(jnp.minimum(z, 126.0))))


def _sb_blocks(qs, k_ref, v_ref, biases, later, jj, carries, accs, *, q_start=None):
    tq, tk = SB_TQ, SB_TK
    ks = pl.ds(pl.multiple_of(jj * tk, tk), tk)
    if q_start is not None:
        mask = (jj * tk + lax.broadcasted_iota(jnp.int32, (tq, tk), 1)
                < q_start + lax.broadcasted_iota(jnp.int32, (tq, tk), 0))
    zs = [_dot_nt(q, k_ref[n, ks, :].astype(BF16)) + bias for n, (q, bias) in enumerate(zip(qs, biases))]
    sps = [_softplus2(z) for z in zs]
    if q_start is not None:
        sps = [jnp.where(mask, sp, 0.0) for sp in sps]
    tails = [_dot(sp.astype(BF16), later) for sp in sps]
    ws = [jnp.exp2((z - sp) - tail - carry) for z, sp, tail, carry in zip(zs, sps, tails, carries)]
    if q_start is not None:
        ws = [jnp.where(mask, w, 0.0) for w in ws]
    accs = [acc + _dot(w.astype(BF16), v_ref[n, ks, :].astype(BF16)) for n, (acc, w) in enumerate(zip(accs, ws))]
    carries = [carry + (tail[:, :1] + sp[:, :1]) for carry, tail, sp in zip(carries, tails, sps)]
    return tuple(carries), tuple(accs)


def _sb_kernel(q_ref, k_ref, v_ref, bias_ref, o_ref):
    i = pl.program_id(2)
    streams = range(SB_STREAMS)
    per_q = SB_TQ // SB_TK
    qs = [(q_ref[:, n * LANES:(n + 1) * LANES] * (LANES ** -0.5 * LOG2E)).astype(BF16) for n in streams]
    biases = [bias_ref[n] for n in streams]
    state = (tuple(jnp.zeros((SB_TQ, 1), F32) for _ in streams),
             tuple(jnp.zeros((SB_TQ, LANES), F32) for _ in streams))
    r = lax.broadcasted_iota(jnp.int32, (SB_TK, SB_TK), 0)
    s = lax.broadcasted_iota(jnp.int32, (SB_TK, SB_TK), 1)
    later = jnp.where(r > s, 1.0, 0.0).astype(BF16)
    for d in range(per_q):
        state = _sb_blocks(qs, k_ref, v_ref, biases, later, (i + 1) * per_q - 1 - d, *state, q_start=i * SB_TQ)

    def body(step, state):
        return _sb_blocks(qs, k_ref, v_ref, biases, later, i * per_q - 1 - step, *state)

    carries, accs = lax.fori_loop(0, i * per_q, body, state)
    for n in streams:
        o_ref[:, n * LANES:(n + 1) * LANES] = accs[n].astype(o_ref.dtype)


def _sb_attention(qc, kc, vc, bias2_keys, B, T, l):
    assert SB_TQ % SB_TK == 0 and T % SB_TQ == 0 and HEADS % SB_STREAMS == 0
    width = SB_STREAMS * LANES
    kv = pl.BlockSpec((None, SB_STREAMS, T, LANES), lambda b, g, i: (b, g, 0, 0))
    qo = pl.BlockSpec((None, SB_TQ, width), lambda b, g, i: (b, i, g))
    return pl.pallas_call(
        _sb_kernel, grid=(B, HEADS // SB_STREAMS, T // SB_TQ),
        in_specs=[qo, kv, kv, pl.BlockSpec((None, SB_STREAMS, 1, SB_TK), lambda b, g, i: (l, g, 0, 0))],
        out_specs=qo, out_shape=jax.ShapeDtypeStruct((B, T, HW), BF16),
        compiler_params=_cparams("parallel", "parallel", "parallel"), name="sb_attention",
    )(qc.reshape(B, T, HW), kc, vc, bias2_keys)


def _sb_decode_kernel(pt_ref, q_ref, bias_ref, *rest):
    del pt_ref
    k_refs = rest[:PAGES_PER_STEP]
    v_refs = rest[PAGES_PER_STEP:2 * PAGES_PER_STEP]
    o_ref, carry_ref = rest[2 * PAGES_PER_STEP:]
    j = pl.program_id(1)

    @pl.when(j == 0)
    def _():
        o_ref[...] = jnp.zeros_like(o_ref)
        carry_ref[...] = jnp.zeros_like(carry_ref)

    qb = (q_ref[...] * (LANES ** -0.5 * LOG2E)).astype(BF16)
    bias = bias_ref[...]
    P = k_refs[0].shape[1]
    head = lax.broadcasted_iota(jnp.int32, (SUBLANES, LANES), 0)
    r = lax.broadcasted_iota(jnp.int32, (P, P), 0)
    s = lax.broadcasted_iota(jnp.int32, (P, P), 1)
    later = jnp.where(r > s, 1.0, 0.0).astype(BF16)
    pages = range(PAGES_PER_STEP)
    zhs = [[_dot_nt(qb, k_refs[p][h].astype(BF16)) for h in range(HEADS)] for p in pages]
    zs = []
    for p in pages:
        z = zhs[p][0]
        for h in range(1, HEADS):
            z = jnp.where(head == h, zhs[p][h], z)
        zs.append(z + bias)
    sps = [_softplus2(z) for z in zs]
    tails = [_dot(sp.astype(BF16), later) for sp in sps]
    carries = [carry_ref[...]]
    for p in pages:
        carries.append(carries[-1] + (tails[p][:, :1] + sps[p][:, :1]))
    ws = [jnp.exp2((zs[p] - sps[p]) - tails[p] - carries[p]).astype(BF16) for p in pages]
    ohs = [[_dot(ws[p], v_refs[p][h].astype(BF16)) for h in range(HEADS)] for p in pages]
    acc = o_ref[...]
    for h in range(HEADS):
        acc = acc + jnp.where(head == h, sum(ohs[p][h] for p in pages), 0.0)
    o_ref[...] = acc
    carry_ref[...] = carries[-1]


def _sb_decode(qc, cache_k, cache_v, page_table, bias2_lanes, l):
    Bs, NP = page_table.shape
    P = cache_k.shape[3]
    assert NP % PAGES_PER_STEP == 0 and P == LANES
    pt = page_table.reshape(-1)
    q = jnp.pad(qc.reshape(Bs, HEADS, LANES), ((0, 0), (0, SUBLANES - HEADS), (0, 0)))

    def page_spec(p):
        def index(b, j, pt_ref):
            return (l, pt_ref[b * NP + (NP - 1 - (j * PAGES_PER_STEP + p))], 0, 0, 0)
        return pl.BlockSpec((None, None, HEADS, P, LANES), index)

    row = pl.BlockSpec((None, SUBLANES, LANES), lambda b, j, pt_ref: (b, 0, 0))
    pages = [page_spec(p) for p in range(PAGES_PER_STEP)]
    out = pl.pallas_call(
        _sb_decode_kernel,
        grid_spec=pltpu.PrefetchScalarGridSpec(
            num_scalar_prefetch=1, grid=(Bs, NP // PAGES_PER_STEP),
            in_specs=[row, pl.BlockSpec((None, SUBLANES, LANES), lambda b, j, pt_ref: (l, 0, 0))] + pages + pages,
            out_specs=row, scratch_shapes=[pltpu.VMEM((SUBLANES, LANES), F32)]),
        out_shape=jax.ShapeDtypeStruct((Bs, SUBLANES, LANES), F32),
        compiler_params=_cparams("parallel", "arbitrary"), name="sb_decode",
    )(pt, q, bias2_lanes, *([cache_k] * PAGES_PER_STEP), *([cache_v] * PAGES_PER_STEP))
    return out[:, :HEADS].reshape(Bs, HW)


def _columns(x):
    pad = jnp.zeros((LANES - x.shape[0], LANES), F32)
    return jnp.concatenate([x, pad], axis=0).T


def _sample_mix_kernel(gu_ref, vn_ref, ws_ref, bs_ref, x_ref, hist_ref, cw_ref, ab_ref, alog_ref, dtb_ref,
                       z_ref, s_ref, go_ref, ya_ref, yb_ref, sn_ref, o_scr):
    Bs = gu_ref.shape[0]
    ya_ref[...] = gu_ref[...] * (ws_ref[...] * vn_ref[...] + bs_ref[...])
    y = cw_ref[CONV_W - 1:CONV_W, :] * x_ref[...]
    for s in range(CONV_W - 1):
        y = y + cw_ref[s:s + 1, :] * hist_ref[s]
    y = _silu(y)
    ab = ab_ref[...]
    decay = jnp.exp(-jnp.exp(alog_ref[...]) * _softplus(ab + dtb_ref[...]))
    beta = _sigmoid(ab)
    for h in range(HEADS):
        hs = slice(h * LANES, (h + 1) * LANES)
        q = _l2n_rows(y[:, hs]) * (LANES ** -0.5)
        k = _l2n_rows(y[:, HW + h * LANES:HW + (h + 1) * LANES])
        v = y[:, 2 * HW + h * LANES:2 * HW + (h + 1) * LANES]
        qcols, kcols = _columns(q), _columns(k)
        for b in range(Bs):
            a = decay[b:b + 1, h:h + 1]
            bt = beta[b:b + 1, HEADS + h:HEADS + h + 1]
            state = s_ref[b, h]
            kcol = kcols[:, b:b + 1]
            ks = jnp.sum(kcol * state, axis=0, keepdims=True)
            u = bt * (v[b:b + 1, :] - a * ks)
            new_state = a * state + kcol * u
            sn_ref[b, h] = new_state
            o_scr[b:b + 1, hs] = jnp.sum(qcols[:, b:b + 1] * new_state, axis=0, keepdims=True)
    for h in range(HEADS):
        hs = slice(h * LANES, (h + 1) * LANES)
        yb_ref[:, hs] = _rms_rows(o_scr[:, hs]) * go_ref[...] * z_ref[:, hs]


def _sample_mix(gu, vn, ws_rows, bs_rows, bqkv, hist, conv_w, zab, alog_rows, dtb_rows, state, go_rows, l):
    Bs = gu.shape[0]
    full = lambda shape: pl.BlockSpec(shape, lambda i: (0,) * len(shape))
    par = lambda shape: pl.BlockSpec((None,) + shape, lambda i: (l,) + (0,) * len(shape))
    ab, zsilu = zab, zab
    return pl.pallas_call(
        _sample_mix_kernel, grid=(1,),
        in_specs=[full((Bs, A_WIDTH)), full((Bs, A_WIDTH)), par((1, A_WIDTH)), par((1, A_WIDTH)),
                  full((Bs, 3 * HW)), par((CONV_W - 1, Bs, 3 * HW)), par((CONV_W, 3 * HW)),
                  pl.BlockSpec((Bs, LANES), lambda i: (0, HW // LANES)), par((1, LANES)), par((1, LANES)),
                  full((Bs, HW)), par((Bs, HEADS, LANES, LANES)), par((1, LANES))],
        out_specs=[full((Bs, A_WIDTH)), full((Bs, HW)), full((Bs, HEADS, LANES, LANES))],
        out_shape=[jax.ShapeDtypeStruct((Bs, A_WIDTH), F32), jax.ShapeDtypeStruct((Bs, HW), F32),
                   jax.ShapeDtypeStruct((Bs, HEADS, LANES, LANES), F32)],
        scratch_shapes=[pltpu.VMEM((Bs, HW), F32)],
        compiler_params=_cparams("arbitrary"), name="sample_mix",
    )(gu, vn, ws_rows, bs_rows, bqkv, hist, conv_w, ab, alog_rows, dtb_rows, zsilu, state, go_rows)


def _merge_kernel(ya_ref, yb_ref, yc_ref, ga_ref, gb_ref, gc_ref, yas_ref, ybs_ref, ycs_ref, gas_ref, gbs_ref, gcs_ref,
                  wa_ref, wb_ref, wc_ref, o_ref, os_ref):
    def merged(ya, yb, yc, ga, gb, gc):
        m = (ga[...] * _dot(ya[...].astype(BF16), wa_ref[...]) + gb[...] * _dot(yb[...].astype(BF16), wb_ref[...])
             + gc[...] * _dot(yc[...].astype(BF16), wc_ref[...]))
        return m.astype(o_ref.dtype)

    o_ref[...] = merged(ya_ref, yb_ref, yc_ref, ga_ref, gb_ref, gc_ref)

    @pl.when(_last_row_step(1))
    def _():
        os_ref[...] = merged(yas_ref, ybs_ref, ycs_ref, gas_ref, gbs_ref, gcs_ref)


def _merge(ys, ys_s, gates, gates_s, wa, wb, wc, l):
    R = ys[0].shape[0]
    Bs = ys_s[0].shape[0]
    D = wa.shape[-1]
    tm = _tile(R, ROW_TILES)
    tn = _tile(D, (1024, 512, 256, LANES))
    nj = D // tn
    rows = lambda width: pl.BlockSpec((tm, width), lambda j, i: (i, 0))
    srows = lambda width: pl.BlockSpec((Bs, width), lambda j, i: (0, 0))
    gate = lambda k: pl.BlockSpec((tm, tn), lambda j, i: (i, k * nj + j))
    sgate = lambda k: pl.BlockSpec((Bs, tn), lambda j, i: (0, k * nj + j))
    wspec = lambda width: pl.BlockSpec((None, width, tn), lambda j, i: (l, 0, j))
    return pl.pallas_call(
        _merge_kernel, grid=(nj, R // tm),
        in_specs=[rows(A_WIDTH), rows(HW), rows(HW), gate(0), gate(1), gate(2),
                  srows(A_WIDTH), srows(HW), srows(HW), sgate(0), sgate(1), sgate(2),
                  wspec(A_WIDTH), wspec(HW), wspec(HW)],
        out_specs=[pl.BlockSpec((tm, tn), lambda j, i: (i, j)), pl.BlockSpec((Bs, tn), lambda j, i: (0, j))],
        out_shape=[jax.ShapeDtypeStruct((R, D), BF16), jax.ShapeDtypeStruct((Bs, D), BF16)],
        compiler_params=_cparams("parallel", "arbitrary"), name="merge",
    )(*ys, gates, gates, gates, *ys_s, gates_s, gates_s, gates_s, wa, wb, wc)


def _outproj_kernel(m_ref, x_ref, ms_ref, xs_ref, w_ref, g_ref, x1_ref, h2_ref, x1s_ref, h2s_ref):
    def project(m, x, x1_out, h2_out):
        x1 = x + _dot(m, w_ref[...])
        x1_out[...] = x1
        h2_out[...] = (_rms_rows(x1) * g_ref[...]).astype(h2_out.dtype)

    project(m_ref[...], x_ref[...], x1_ref, h2_ref)

    @pl.when(_last_row_step(0))
    def _():
        project(ms_ref[...], xs_ref[...], x1s_ref, h2s_ref)


def _outproj(m, x, m_s, x_s, w_out, gains, l):
    R, D = x.shape
    Bs = x_s.shape[0]
    tm = _tile(R, ROW_TILES[1:])
    rows = pl.BlockSpec((tm, D), lambda i: (i, 0))
    srows = pl.BlockSpec((Bs, D), lambda i: (0, 0))
    return pl.pallas_call(
        _outproj_kernel, grid=(R // tm,),
        in_specs=[rows, rows, srows, srows, pl.BlockSpec((None, D, D), lambda i: (l, 0, 0)),
                  pl.BlockSpec((None, 1, D), lambda i: (l, 0, 0))],
        out_specs=[rows, rows, srows, srows],
        out_shape=[jax.ShapeDtypeStruct((R, D), F32), jax.ShapeDtypeStruct((R, D), BF16),
                   jax.ShapeDtypeStruct((Bs, D), F32), jax.ShapeDtypeStruct((Bs, D), BF16)],
        compiler_params=_cparams("arbitrary"), name="outproj")(m, x, m_s, x_s, w_out, gains)


def _ffn_up_kernel(h_ref, hs_ref, wg_ref, wu_ref, o_ref, os_ref, wgb_ref, wub_ref):
    @pl.when(pl.program_id(1) == 0)
    def _():
        wgb_ref[...] = wg_ref[...].astype(BF16)
        wub_ref[...] = wu_ref[...].astype(BF16)

    def swiglu(h):
        return _silu(_dot(h, wgb_ref[...])) * _dot(h, wub_ref[...])

    o_ref[...] = swiglu(h_ref[...]).astype(o_ref.dtype)

    @pl.when(_last_row_step(1))
    def _():
        os_ref[...] = swiglu(hs_ref[...]).astype(os_ref.dtype)


def _ffn_up(h2, h2_s, w_gate, w_up, l):
    R, D = h2.shape
    Bs = h2_s.shape[0]
    F = w_gate.shape[-1]
    tm = _tile(R, ROW_TILES)
    tn = _tile(F, (512, 256, LANES))
    wspec = pl.BlockSpec((None, D, tn), lambda j, i: (l, 0, j))
    return pl.pallas_call(
        _ffn_up_kernel, grid=(F // tn, R // tm),
        in_specs=[pl.BlockSpec((tm, D), lambda j, i: (i, 0)), pl.BlockSpec((Bs, D), lambda j, i: (0, 0)),
                  wspec, wspec],
        out_specs=[pl.BlockSpec((tm, tn), lambda j, i: (i, j)), pl.BlockSpec((Bs, tn), lambda j, i: (0, j))],
        out_shape=[jax.ShapeDtypeStruct((R, F), BF16), jax.ShapeDtypeStruct((Bs, F), BF16)],
        scratch_shapes=[pltpu.VMEM((D, tn), BF16), pltpu.VMEM((D, tn), BF16)],
        compiler_params=_cparams("arbitrary", "arbitrary"), name="ffn_up")(h2, h2_s, w_gate, w_up)


def _ffn_down_kernel(a_ref, x_ref, as_ref, xs_ref, w_ref, o_ref, os_ref, wb_ref):
    @pl.when(pl.program_id(1) == 0)
    def _():
        wb_ref[...] = w_ref[...].astype(BF16)

    o_ref[...] = x_ref[...] + _dot(a_ref[...], wb_ref[...])

    @pl.when(_last_row_step(1))
    def _():
        os_ref[...] = xs_ref[...] + _dot(as_ref[...], wb_ref[...])


def _ffn_down(act, x1, act_s, x1_s, w_down, l):
    R, F = act.shape
    Bs = act_s.shape[0]
    D = x1.shape[-1]
    tm = _tile(R, ROW_TILES[1:])
    tn = _tile(D, (512, 256, LANES))
    tile = pl.BlockSpec((tm, tn), lambda j, i: (i, j))
    stile = pl.BlockSpec((Bs, tn), lambda j, i: (0, j))
    return pl.pallas_call(
        _ffn_down_kernel, grid=(D // tn, R // tm),
        in_specs=[pl.BlockSpec((tm, F), lambda j, i: (i, 0)), tile,
                  pl.BlockSpec((Bs, F), lambda j, i: (0, 0)), stile,
                  pl.BlockSpec((None, F, tn), lambda j, i: (l, 0, j))],
        out_specs=[tile, stile],
        out_shape=[jax.ShapeDtypeStruct((R, D), F32), jax.ShapeDtypeStruct((Bs, D), F32)],
        scratch_shapes=[pltpu.VMEM((F, tn), BF16)],
        compiler_params=_cparams("arbitrary", "arbitrary"), name="ffn_down")(act, x1, act_s, x1_s, w_down)


def _prepare_params(d_model, norm_mix, w_in, a_vnorm, a_ws, a_bs, dn_conv, dn_a_log, dn_dt_bias, dn_onorm,
                    sb_qnorm, sb_knorm, sb_bias, w_branch, w_out, norm_ffn, w_gate, w_up, w_down):
    depth = w_in.shape[0]
    sizes = (A_WIDTH, A_WIDTH, 3 * HW, HW + 2 * HEADS, HW, HW, HW, 3 * d_model)
    names = ("a_u", "a_v", "b_qkv", "b_zab", "c_q", "c_k", "c_v", "gates")
    p, lo = {}, 0
    for name, size in zip(names, sizes, strict=True):
        p["w_" + name] = w_in[:, :, lo:lo + size].astype(BF16)
        lo += size
    p["w_b_zab"] = jnp.pad(p["w_b_zab"], ((0, 0), (0, 0), (0, ZAB_WIDTH - HW - 2 * HEADS)))
    pad_heads = lambda a: jnp.pad(a, ((0, 0), (0, LANES - HEADS)))[:, None, :]
    tile_heads = lambda a: jnp.tile(a, (1, HEADS))[:, None, :]
    p.update(
        norm_mix=norm_mix[:, None, :], norm_ffn=norm_ffn[:, None, :],
        a_vnorm=a_vnorm.reshape(depth, 1, A_WIDTH), a_ws=a_ws,
        a_bs_rows=jnp.repeat(jnp.swapaxes(a_bs, 1, 2), LANES, axis=2),
        a_ws0=jnp.repeat(a_ws[:, :, 0, 0], LANES, axis=1)[:, None, :],
        a_bs0=jnp.repeat(a_bs[:, :, 0], LANES, axis=1)[:, None, :],
        dn_conv=dn_conv, alog=pad_heads(dn_a_log), dtb=pad_heads(dn_dt_bias),
        dn_onorm=dn_onorm[:, None, :], sb_qnorm=tile_heads(sb_qnorm), sb_knorm=tile_heads(sb_knorm),
        sb_bias2_keys=jnp.broadcast_to((sb_bias * LOG2E)[:, :, None, None], sb_bias.shape + (1, SB_TK)),
        sb_bias2_lanes=jnp.broadcast_to(jnp.pad(sb_bias * LOG2E, ((0, 0), (0, SUBLANES - HEADS)))[:, :, None],
                                        (depth, SUBLANES, LANES)),
        wb_a=w_branch[:, :A_WIDTH].astype(BF16), wb_b=w_branch[:, A_WIDTH:A_WIDTH + HW].astype(BF16),
        wb_c=w_branch[:, A_WIDTH + HW:].astype(BF16), w_out=w_out.astype(BF16),
        w_gate=w_gate, w_up=w_up, w_down=w_down)
    return p


def _project(x, xs, p, l, B, T, kv_stacks):
    depth = p["norm_mix"].shape[0]
    h, hs = _rms_cast(x, xs, p["norm_mix"], l)
    epi = lambda **kw: functools.partial(_epilogue, **kw)
    calls = dict(
        gu=_proj(h, hs, p["w_a_u"], l, epi(act=jax.nn.gelu)),
        vn=_proj(h, hs, p["w_a_v"], l, epi(act=jax.nn.gelu, group_rms=True), p["a_vnorm"], out_dtypes=(BF16,)),
        bqkv=_proj(h, hs, p["w_b_qkv"], l, epi()),
        zab=_proj(h, hs, p["w_b_zab"], l, epi(act=_silu, act_cols=HW)),
        qc=_proj(h, hs, p["w_c_q"], l, epi(group_rms=True), p["sb_qnorm"]),
        kc=_proj(h, hs, p["w_c_k"], l, epi(group_rms=True), p["sb_knorm"], head_major=(B, T), out_dtypes=(F32, BF16),
                 layer_stack=(depth, kv_stacks[0])),
        vc=_proj(h, hs, p["w_c_v"], l, epi(), head_major=(B, T), out_dtypes=(F32, BF16),
                 layer_stack=(depth, kv_stacks[1])),
        gates=_proj(h, hs, p["w_gates"], l, epi(act=_sigmoid), out_dtypes=(BF16,)))
    prompt = {k: v[0] for k, v in calls.items()}
    prompt.update(kc_bf16=calls["kc"][1], vc_bf16=calls["vc"][1])
    return prompt, {k: v[-1] for k, v in calls.items()}


def _layer(x, xs, p, l, B, T, kv_stacks, cache_k, cache_v, page_table, state_delta, state_conv):
    Bs = xs.shape[0]
    s, ss = _project(x, xs, p, l, B, T, kv_stacks)
    ya = _mixer_a(s["gu"], s["vn"], p["a_ws"], p["a_bs_rows"], l)
    q, k, v, gbeta, grow = _gdn_prep(s["bqkv"], s["zab"], p["dn_conv"], p["alog"], p["dtb"], B, T, l)
    yb, state = _gdn(q, k, v, gbeta, grow, s["zab"], p["dn_onorm"], B, T, l)
    yc = _sb_attention(s["qc"], s["kc_bf16"], s["vc_bf16"], p["sb_bias2_keys"], B, T, l)
    hist = jnp.swapaxes(state_conv, 1, 2)
    ya_s, yb_s, state_s = _sample_mix(ss["gu"], ss["vn"], p["a_ws0"], p["a_bs0"], ss["bqkv"], hist, p["dn_conv"],
                                      ss["zab"], p["alog"], p["dtb"], state_delta, p["dn_onorm"], l)
    yc_s = _sb_decode(ss["qc"], cache_k, cache_v, page_table, p["sb_bias2_lanes"], l)
    m, m_s = _merge((ya, yb.reshape(B * T, HW), yc.reshape(B * T, HW)), (ya_s, yb_s, yc_s), s["gates"], ss["gates"],
                    p["wb_a"], p["wb_b"], p["wb_c"], l)
    x1, h2, x1_s, h2_s = _outproj(m, x, m_s, xs, p["w_out"], p["norm_ffn"], l)
    act, act_s = _ffn_up(h2, h2_s, p["w_gate"], p["w_up"], l)
    y, y_s = _ffn_down(act, x1, act_s, x1_s, p["w_down"], l)
    conv_tail = s["bqkv"].reshape(B, T, 3 * HW)[:, T - (CONV_W - 1):]
    conv_s = jnp.concatenate([state_conv[l][:, 1:], ss["bqkv"][:, None, :]], axis=1)
    sample_outs = (ss["kc"].reshape(Bs, 1, HEADS, LANES), ss["vc"].reshape(Bs, 1, HEADS, LANES), state_s, conv_s,
                   ss["vn"][:, None, :])
    return y, y_s, (s["kc"], s["vc"]), (state, conv_tail), sample_outs


def kernel(x_prompt, x_sample, cache_k, cache_v, state_delta, state_conv, page_table, norm_mix, w_in, a_vnorm, a_ws, a_bs, dn_conv, dn_a_log, dn_dt_bias, dn_onorm, sb_qnorm, sb_knorm, sb_bias, w_branch, w_out, norm_ffn, w_gate, w_up, w_down):
    B, T, D = x_prompt.shape
    Bs = x_sample.shape[0]
    depth = w_in.shape[0]
    assert x_sample.shape[1] == 1 and T % SB_TQ == 0 and T % GDN_CHUNK == 0
    p = _prepare_params(D, norm_mix, w_in, a_vnorm, a_ws, a_bs, dn_conv, dn_a_log, dn_dt_bias, dn_onorm,
                        sb_qnorm, sb_knorm, sb_bias, w_branch, w_out, norm_ffn, w_gate, w_up, w_down)
    cache_k = jnp.swapaxes(cache_k, 2, 3)
    cache_v = jnp.swapaxes(cache_v, 2, 3)
    yp = x_prompt.reshape(B * T, D)
    ys = x_sample.reshape(Bs, D)
    prompt_outs, sample_outs = [], []
    kv_stacks = (None, None)
    for l in range(depth):
        yp, ys, kv_stacks, outs, outs_s = _layer(yp, ys, p, l, B, T, kv_stacks, cache_k, cache_v, page_table,
                                                 state_delta, state_conv)
        prompt_outs.append(outs)
        sample_outs.append(outs_s)
    stack = lambda outs, i: jnp.stack([o[i] for o in outs])
    by_token = lambda a: jnp.swapaxes(a, 2, 3)
    return (yp.reshape(B, T, D), ys.reshape(Bs, 1, D),
            by_token(kv_stacks[0]), by_token(kv_stacks[1]), stack(prompt_outs, 0), stack(prompt_outs, 1),
            stack(sample_outs, 0), stack(sample_outs, 1), stack(sample_outs, 2), stack(sample_outs, 3),
            stack(sample_outs, 4))
```
